```python
import math
import jax
import jax.numpy as jnp
from jax import lax
import numpy as np

D_MODEL = 2048
BATCH = 2
SEQ = 4096
DEPTH = 4
DEC_BATCH = 8
DEC_SEQ = 1
PAST_LEN = 16384
PAGE_SIZE = 128

N_MIXERS = 2
N_RWKV_LAYERS = (DEPTH + 1) // N_MIXERS
MIX_WIDTH = D_MODEL
MEM_LEN = 256
C_HEADS = 4
C_HEAD_DIM = 128
C_WIDTH = C_HEADS * C_HEAD_DIM
A_WIDTH = MIX_WIDTH - C_WIDTH
A_HEAD_DIM = 64
A_HEADS = A_WIDTH // A_HEAD_DIM
DECAY_LORA = 96
ICLR_LORA = 96
VRES_LORA = 64
GN_EPS = 64e-5
RW_COLS_FIRST = 3 * A_WIDTH + DECAY_LORA + ICLR_LORA
RW_COLS = RW_COLS_FIRST + VRES_LORA
B_WIDTH = MIX_WIDTH - C_WIDTH
B_HEAD_DIM = 128
B_HEADS = B_WIDTH // B_HEAD_DIM
MOBA_BLOCK = 256
MOBA_TOPK = 3
QUERY_CHUNK = 16
RMS_EPS = 1e-6

kernel_name = 'rwkv7_moba_hybrid_decode_step'


def rms_norm(x, g):
    xf = x.astype(jnp.float32)
    y = xf * lax.rsqrt(jnp.mean(xf * xf, axis=-1, keepdims=True) + RMS_EPS)
    return (y * g.astype(jnp.float32)).astype(x.dtype)


def split_heads(t, n_heads):
    return t.reshape(t.shape[:-1] + (n_heads, t.shape[-1] // n_heads))


def alibi_slopes(n):
    def pow2_slopes(m):
        start = 2.0 ** (-(2.0 ** -(math.log2(m) - 3)))
        return [start ** (i + 1) for i in range(m)]
    if math.log2(n).is_integer():
        s = pow2_slopes(n)
    else:
        c = 2 ** math.floor(math.log2(n))
        s = pow2_slopes(c) + pow2_slopes(2 * c)[0::2][: n - c]
    return np.asarray(s, dtype=np.float32)


def memory_kv(mem, w_mem_kv):
    k, v = jnp.split(mem @ w_mem_kv, 2, axis=-1)
    return split_heads(k, C_HEADS), split_heads(v, C_HEADS)


def memory_attention(q, mem_k, mem_v):
    qh = split_heads(q, C_HEADS)
    s = jnp.einsum('bthd,bmhd->bhtm', qh, mem_k).astype(jnp.float32) * (C_HEAD_DIM ** -0.5)
    p = jax.nn.softmax(s, axis=-1).astype(mem_v.dtype)
    return jnp.einsum('bhtm,bmhd->bthd', p, mem_v).reshape(q.shape)


def rwkv7_scan(r, w, k, v, a, b, s0):
    def step(S, inp):
        r_t, w_t, k_t, v_t, a_t, b_t = inp
        sa = jnp.einsum('bhij,bhj->bhi', S, a_t)
        S = S * w_t[:, :, None, :] + sa[..., None] * b_t[:, :, None, :] + v_t[..., None] * k_t[:, :, None, :]
        return S, jnp.einsum('bhij,bhj->bhi', S, r_t)
    xs = tuple(jnp.moveaxis(t, 1, 0) for t in (r, w, k, v, a, b))
    S, ys = lax.scan(step, s0, xs)
    return jnp.moveaxis(ys, 0, 1), S


def rwkv7_mixer(z_rw, z_prev, wkv0, v_first, mu, decay_b, decay_0, iclr_b, iclr_0,
                k_k, k_a, r_k, gn_w, gn_b, vres_b, vres_0):
    f32 = jnp.float32
    B, T, _ = z_rw.shape
    W = A_WIDTH
    zs = z_rw + mu * (z_prev - z_rw)
    r, k, v = zs[..., :W], zs[..., W:2 * W], zs[..., 2 * W:3 * W]
    o = 3 * W
    z_w = zs[..., o:o + DECAY_LORA]
    o += DECAY_LORA
    z_a = zs[..., o:o + ICLR_LORA]
    o += ICLR_LORA
    log_w = -jax.nn.softplus(-(decay_0 + jnp.tanh(z_w) @ decay_b)) - 0.5
    a = jax.nn.sigmoid(iclr_0 + z_a @ iclr_b)
    if vres_b is None:
        v_first = v
    else:
        z_v = zs[..., o:o + VRES_LORA]
        v = v + (v_first - v) * jax.nn.sigmoid(vres_0 + z_v @ vres_b)
    kk = split_heads((k * k_k).astype(f32), A_HEADS)
    kk = kk / jnp.maximum(jnp.sqrt(jnp.sum(kk * kk, axis=-1, keepdims=True)), 1e-12)
    k = k * (1 + (a - 1) * k_a)
    rh, kh, vh, ah = (split_heads(t.astype(f32), A_HEADS) for t in (r, k, v, a))
    decay = jnp.exp(-jnp.exp(split_heads(log_w.astype(f32), A_HEADS)))
    y, S = rwkv7_scan(rh, decay, kh, vh, -kk, kk * ah, wkv0.astype(f32))
    mean = jnp.mean(y, axis=-1, keepdims=True)
    var = jnp.mean(jnp.square(y - mean), axis=-1, keepdims=True)
    y = ((y - mean) * lax.rsqrt(var + GN_EPS)).reshape(B, T, W) * gn_w + gn_b
    y = y + (jnp.sum(rh * kh * r_k.astype(f32), axis=-1, keepdims=True) * vh).reshape(B, T, W)
    return y.astype(z_rw.dtype), v_first, S


def moba_attention(q, k_all, v_all, q_start):
    f32 = jnp.float32
    B, Tq, H, Dh = q.shape
    Lk = k_all.shape[1]
    n_blk = -(-Lk // MOBA_BLOCK)
    n_sel = min(MOBA_TOPK, n_blk)
    pad = ((0, 0), (0, n_blk * MOBA_BLOCK - Lk), (0, 0), (0, 0))
    kb = jnp.pad(k_all, pad).reshape(B, n_blk, MOBA_BLOCK, H, Dh)
    vb = jnp.pad(v_all, pad).reshape(B, n_blk, MOBA_BLOCK, H, Dh)
    k_mean = jnp.mean(kb, axis=2, dtype=f32)
    slopes = jnp.asarray(alibi_slopes(H))
    scale = Dh ** -0.5
    b_idx = jnp.arange(B)[:, None, None, None]
    h_idx = jnp.arange(H)[None, :, None, None]
    offs = jnp.arange(MOBA_BLOCK)

    def attend(args):
        qc, pos = args
        Tc = qc.shape[1]
        own = pos // MOBA_BLOCK
        gate = jnp.einsum('bthd,bnhd->bhtn', qc.astype(f32), k_mean)
        fully_past = jnp.arange(n_blk)[None, :] < own[:, None]
        gate = jnp.where(fully_past, gate, -jnp.inf)
        _, top = lax.top_k(gate, n_sel)
        blk = jnp.concatenate([top, jnp.broadcast_to(own[None, None, :, None], (B, H, Tc, 1))], axis=-1)
        slot_ok = jnp.concatenate([jnp.arange(n_sel)[None, :] < own[:, None],
                                   jnp.ones((Tc, 1), dtype=bool)], axis=-1)
        kg = kb[b_idx, blk, :, h_idx]
        vg = vb[b_idx, blk, :, h_idx]
        kpos = blk[..., None] * MOBA_BLOCK + offs
        qpos = pos[None, None, :, None, None]
        ok = slot_ok[None, None, :, :, None] & (kpos <= qpos)
        logits = (jnp.einsum('bthd,bhtjsd->bhtjs', qc, kg).astype(f32) * scale
                  - slopes[None, :, None, None, None] * (qpos - kpos).astype(f32))
        logits = jnp.where(ok, logits, -jnp.inf).reshape(B, H, Tc, -1)
        p = jax.nn.softmax(logits, axis=-1).astype(vg.dtype)
        return jnp.einsum('bhtn,bhtnd->bthd', p, vg.reshape(B, H, Tc, -1, Dh))

    pos = q_start + jnp.arange(Tq)
    if Tq > QUERY_CHUNK and Tq % QUERY_CHUNK == 0:
        n = Tq // QUERY_CHUNK
        qs = jnp.moveaxis(q.reshape(B, n, QUERY_CHUNK, H, Dh), 1, 0)
        out = lax.map(attend, (qs, pos.reshape(n, QUERY_CHUNK)))
        return jnp.moveaxis(out, 0, 1).reshape(B, Tq, H, Dh)
    return attend((q, pos))


def gather_pages(pool, page_table):
    pages = pool[page_table]
    return pages.reshape(page_table.shape[0], -1, pool.shape[2], pool.shape[3])


def run_trunk(x, mem_kv, shift_prev, wkv0, kv_past, norm_pre, norm_post, w_in, w_out, rwkv_params):
    B, T, _ = x.shape
    v_first = None
    new_wkv, new_shift, new_kv = [], [], []
    for i in range(DEPTH):
        j = i // N_MIXERS
        h = rms_norm(x, norm_pre[i])
        z = h @ w_in[i]
        if i % N_MIXERS == 0:
            p = rwkv_params[j]
            n_mix = p['mu'].shape[0]
            z_rw = z[..., :n_mix]
            z_first_prev = (shift_prev[j] @ w_in[i][:, :n_mix])[:, None, :]
            z_prev = jnp.concatenate([z_first_prev, z_rw[:, :-1]], axis=1)
            y_mix, v_first, s_new = rwkv7_mixer(z_rw, z_prev, wkv0[j], v_first, **p)
            new_wkv.append(s_new.astype(x.dtype))
            new_shift.append(h[:, -1])
        else:
            n_mix = 3 * B_WIDTH
            qh, kh, vh = (split_heads(z[..., n * B_WIDTH:(n + 1) * B_WIDTH], B_HEADS) for n in range(3))
            if kv_past is None:
                k_all, v_all, q_start = kh, vh, 0
            else:
                k_past, v_past = kv_past[j]
                q_start = k_past.shape[1]
                k_all = jnp.concatenate([k_past.astype(kh.dtype), kh], axis=1)
                v_all = jnp.concatenate([v_past.astype(vh.dtype), vh], axis=1)
            y_mix = moba_attention(qh, k_all, v_all, q_start).reshape(B, T, B_WIDTH)
            new_kv.append((kh, vh))
        rest = z[..., n_mix:]
        q_mem, gate = rest[..., :C_WIDTH], rest[..., C_WIDTH:]
        y_mem = memory_attention(q_mem, mem_kv[i][0], mem_kv[i][1])
        y = jnp.concatenate([y_mix, y_mem.astype(y_mix.dtype)], axis=-1) * jax.nn.silu(gate)
        x = x + rms_norm(y @ w_out[i], norm_post[i])
    return x, jnp.stack(new_wkv), jnp.stack(new_shift), new_kv


def setup_inputs(seed: int = 0) -> dict:
    key = jax.random.key(seed)
    keys = iter(jax.random.split(key, 48))
    f32 = jnp.float32

    def normal(shape, scale=1.0):
        return jax.random.normal(next(keys), shape, f32) * scale

    def uniform(shape, lo, hi):
        return jax.random.uniform(next(keys), shape, f32, lo, hi)

    n_pages = PAST_LEN // PAGE_SIZE
    n_pool = (5 * DEC_BATCH * n_pages + 3) // 4
    win_a0 = RW_COLS_FIRST + C_WIDTH + MIX_WIDTH
    win_a = RW_COLS + C_WIDTH + MIX_WIDTH
    win_b = 3 * B_WIDTH + C_WIDTH + MIX_WIDTH
    d_scale = D_MODEL ** -0.5
    page_table = jax.random.permutation(next(keys), n_pool)[: DEC_BATCH * n_pages]
    page_table = page_table.reshape(DEC_BATCH, n_pages).astype(jnp.int32)
    return {
        'x_prompt': normal((BATCH, SEQ, D_MODEL)),
        'x_sample': normal((DEC_BATCH, DEC_SEQ, D_MODEL)),
        'state_wkv': normal((N_RWKV_LAYERS, DEC_BATCH, A_HEADS, A_HEAD_DIM, A_HEAD_DIM), 0.3),
        'state_shift': normal((N_RWKV_LAYERS, DEC_BATCH, D_MODEL)),
        'cache_k_l1': normal((n_pool, PAGE_SIZE, B_HEADS, B_HEAD_DIM)),
        'cache_v_l1': normal((n_pool, PAGE_SIZE, B_HEADS, B_HEAD_DIM)),
        'cache_k_l3': normal((n_pool, PAGE_SIZE, B_HEADS, B_HEAD_DIM)),
        'cache_v_l3': normal((n_pool, PAGE_SIZE, B_HEADS, B_HEAD_DIM)),
        'cache_mem_k': normal((DEPTH, DEC_BATCH, MEM_LEN, C_HEADS, C_HEAD_DIM)),
        'cache_mem_v': normal((DEPTH, DEC_BATCH, MEM_LEN, C_HEADS, C_HEAD_DIM)),
        'page_table': page_table,
        'mem_prompt': normal((BATCH, MEM_LEN, D_MODEL)),
        'norm_pre': 1.0 + normal((DEPTH, D_MODEL), 0.05),
        'norm_post': 1.0 + normal((DEPTH, D_MODEL), 0.05),
        'w_in_l0': normal((D_MODEL, win_a0), d_scale),
        'w_in_l1': normal((D_MODEL, win_b), d_scale),
        'w_in_l2': normal((D_MODEL, win_a), d_scale),
        'w_in_l3': normal((D_MODEL, win_b), d_scale),
        'w_mem_kv': normal((DEPTH, D_MODEL, 2 * C_WIDTH), d_scale),
        'w_out': normal((DEPTH, MIX_WIDTH, D_MODEL), MIX_WIDTH ** -0.5),
        'mu_l0': uniform((RW_COLS_FIRST,), 0.0, 1.0),
        'mu_l2': uniform((RW_COLS,), 0.0, 1.0),
        'rwkv_decay_b': normal((N_RWKV_LAYERS, DECAY_LORA, A_WIDTH), 0.1 * DECAY_LORA ** -0.5),
        'rwkv_decay_0': uniform((N_RWKV_LAYERS, A_WIDTH), -6.5, -1.5),
        'rwkv_iclr_b': normal((N_RWKV_LAYERS, ICLR_LORA, A_WIDTH), 0.1 * ICLR_LORA ** -0.5),
        'rwkv_iclr_0': normal((N_RWKV_LAYERS, A_WIDTH), 0.1),
        'rwkv_k_k': 0.85 + normal((N_RWKV_LAYERS, A_WIDTH), 0.05),
        'rwkv_k_a': 1.0 + normal((N_RWKV_LAYERS, A_WIDTH), 0.05),
        'rwkv_r_k': normal((N_RWKV_LAYERS, A_HEADS, A_HEAD_DIM), 0.1),
        'rwkv_gn_w': 1.0 + normal((N_RWKV_LAYERS, A_WIDTH), 0.05),
        'rwkv_gn_b': normal((N_RWKV_LAYERS, A_WIDTH), 0.01),
        'vres_b_l2': normal((VRES_LORA, A_WIDTH), 0.1 * VRES_LORA ** -0.5),
        'vres_0_l2': 1.0 + normal((A_WIDTH,), 0.1),
    }


def reference(x_prompt, x_sample, state_wkv, state_shift, cache_k_l1, cache_v_l1, cache_k_l3, cache_v_l3,
              cache_mem_k, cache_mem_v, page_table, mem_prompt, norm_pre, norm_post,
              w_in_l0, w_in_l1, w_in_l2, w_in_l3, w_mem_kv, w_out, mu_l0, mu_l2,
              rwkv_decay_b, rwkv_decay_0, rwkv_iclr_b, rwkv_iclr_0, rwkv_k_k, rwkv_k_a, rwkv_r_k,
              rwkv_gn_w, rwkv_gn_b, vres_b_l2, vres_0_l2):
    w_in = [w_in_l0, w_in_l1, w_in_l2, w_in_l3]
    mus = [mu_l0, mu_l2]
    vres = [(None, None), (vres_b_l2, vres_0_l2)]
    rwkv_params = [dict(mu=mus[j], decay_b=rwkv_decay_b[j], decay_0=rwkv_decay_0[j],
                        iclr_b=rwkv_iclr_b[j], iclr_0=rwkv_iclr_0[j], k_k=rwkv_k_k[j], k_a=rwkv_k_a[j],
                        r_k=rwkv_r_k[j], gn_w=rwkv_gn_w[j], gn_b=rwkv_gn_b[j],
                        vres_b=vres[j][0], vres_0=vres[j][1]) for j in range(N_RWKV_LAYERS)]

    bp = x_prompt.shape[0]
    mem_kv_prompt = [memory_kv(mem_prompt, w_mem_kv[i]) for i in range(DEPTH)]
    zero_shift = jnp.zeros((N_RWKV_LAYERS, bp, D_MODEL), x_prompt.dtype)
    zero_wkv = jnp.zeros((N_RWKV_LAYERS, bp, A_HEADS, A_HEAD_DIM, A_HEAD_DIM), jnp.float32)
    y_prompt, p_wkv, p_shift, p_kv = run_trunk(x_prompt, mem_kv_prompt, zero_shift, zero_wkv, None,
                                               norm_pre, norm_post, w_in, w_out, rwkv_params)
    (p_k_l1, p_v_l1), (p_k_l3, p_v_l3) = p_kv
    p_mem_k = jnp.stack([kv[0] for kv in mem_kv_prompt])
    p_mem_v = jnp.stack([kv[1] for kv in mem_kv_prompt])

    mem_kv_sample = [(cache_mem_k[i], cache_mem_v[i]) for i in range(DEPTH)]
    kv_past = [(gather_pages(cache_k_l1, page_table), gather_pages(cache_v_l1, page_table)),
               (gather_pages(cache_k_l3, page_table), gather_pages(cache_v_l3, page_table))]
    y_sample, s_wkv, s_shift, s_kv = run_trunk(x_sample, mem_kv_sample, state_shift, state_wkv, kv_past,
                                               norm_pre, norm_post, w_in, w_out, rwkv_params)
    (s_k_l1, s_v_l1), (s_k_l3, s_v_l3) = s_kv
    return (y_prompt, y_sample, p_wkv, p_shift, p_k_l1, p_v_l1, p_k_l3, p_v_l3, p_mem_k, p_mem_v,
            s_wkv, s_shift, s_k_l1, s_v_l1, s_k_l3, s_v_l3)
```

```python
import functools
import math

import numpy as np
import jax
import jax.numpy as jnp
from jax import lax
from jax.experimental import pallas as pl
from jax.experimental.pallas import tpu as pltpu

F32 = jnp.float32
BF16 = jnp.bfloat16

RMS_EPS = 1e-6
GN_EPS = 64e-5
A_HEAD_DIM = 64
B_HEAD_DIM = 128
C_HEADS = 4
C_HEAD_DIM = 128
C_WIDTH = C_HEADS * C_HEAD_DIM
MOBA_BLOCK = 256
MOBA_TOPK = 3
DECAY_LORA = 96
ICLR_LORA = 96
VRES_LORA = 64

LANES = 128
RWKV_CHUNK = 64
LORA_PAD = 512
NEG_BIG = -1e30
VMEM_LIMIT = 48 * 1024 * 1024


def _cparams(sem):
    return pltpu.CompilerParams(dimension_semantics=sem, vmem_limit_bytes=VMEM_LIMIT)


def alibi_slopes(n):
    def pow2_slopes(m):
        start = 2.0 ** (-(2.0 ** -(math.log2(m) - 3)))
        return [start ** (i + 1) for i in range(m)]
    if math.log2(n).is_integer():
        s = pow2_slopes(n)
    else:
        c = 2 ** math.floor(math.log2(n))
        s = pow2_slopes(c) + pow2_slopes(2 * c)[0::2][: n - c]
    return np.asarray(s, dtype=np.float32)


def _bdot(a, b):
    return jnp.dot(a.astype(BF16), b.astype(BF16), preferred_element_type=F32)


def _bdot_nt(a, b):
    return lax.dot_general(a.astype(BF16), b.astype(BF16), (((1,), (1,)), ((), ())),
                           preferred_element_type=F32)


def _mm(a, b, precise):
    if precise:
        return jnp.dot(a, b, preferred_element_type=F32, precision=lax.Precision.HIGHEST)
    return _bdot(a, b)


def _mm_nt(a, b, precise):
    if precise:
        return lax.dot_general(a, b, (((1,), (1,)), ((), ())), preferred_element_type=F32,
                               precision=lax.Precision.HIGHEST)
    return _bdot_nt(a, b)


def _split3(x):
    hi = x.astype(BF16)
    r1 = x - hi.astype(F32)
    mid = r1.astype(BF16)
    lo = (r1 - mid.astype(F32)).astype(BF16)
    return hi, mid, lo


def _dot_rhs01(x, q01):
    hi, mid, lo = _split3(x)
    acc = jnp.dot(lo, q01, preferred_element_type=F32)
    acc = acc + jnp.dot(mid, q01, preferred_element_type=F32)
    return acc + jnp.dot(hi, q01, preferred_element_type=F32)


def _dot_lhs01(t01, x):
    hi, mid, lo = _split3(x)
    acc = jnp.dot(t01, lo, preferred_element_type=F32)
    acc = acc + jnp.dot(t01, mid, preferred_element_type=F32)
    return acc + jnp.dot(t01, hi, preferred_element_type=F32)


def _dot_nt_x3(a, b):
    ah = a.astype(BF16)
    al = (a - ah.astype(F32)).astype(BF16)
    bh = b.astype(BF16)
    bl = (b - bh.astype(F32)).astype(BF16)
    dn = (((1,), (1,)), ((), ()))
    acc = lax.dot_general(al, bh, dn, preferred_element_type=F32)
    acc = acc + lax.dot_general(ah, bl, dn, preferred_element_type=F32)
    return acc + lax.dot_general(ah, bh, dn, preferred_element_type=F32)


def _head_sum_matrix(width, head):
    r = lax.broadcasted_iota(jnp.int32, (width, width), 0) // head
    c = lax.broadcasted_iota(jnp.int32, (width, width), 1) // head
    return (r == c).astype(BF16)


def _sigmoid(x):
    return 1.0 / (1.0 + jnp.exp(-x))


def _softplus(x):
    return jnp.maximum(x, 0.0) + jnp.log(1.0 + jnp.exp(-jnp.abs(x)))


def _rms_scale(x, g):
    ms = jnp.mean(x * x, axis=-1, keepdims=True)
    return x * lax.rsqrt(ms + RMS_EPS) * g


def _norm_matmul_kernel(x_ref, g_ref, w_ref, o_ref, h_ref, *, normalize, row_block, precise):
    @pl.when(pl.program_id(1) == 0)
    def _():
        tm = x_ref.shape[0]
        for s in range(0, tm, row_block):
            x = x_ref[s:s + row_block, :]
            if normalize:
                x = _rms_scale(x, g_ref[...])
            h_ref[s:s + row_block, :] = x.astype(h_ref.dtype)
    o_ref[...] = _mm(h_ref[...], w_ref[...], precise)


def norm_matmul(x, g, w, *, normalize, tm, tn):
    m, k = x.shape
    n = w.shape[1]
    assert m % tm == 0 and n % tn == 0, (m, tm, n, tn)
    row_block = min(tm, 256)
    precise = w.dtype == F32
    return pl.pallas_call(
        functools.partial(_norm_matmul_kernel, normalize=normalize, row_block=row_block, precise=precise),
        grid=(m // tm, n // tn),
        in_specs=[pl.BlockSpec((tm, k), lambda i, j: (i, 0)),
                  pl.BlockSpec((1, k), lambda i, j: (0, 0)),
                  pl.BlockSpec((k, tn), lambda i, j: (0, j))],
        out_specs=pl.BlockSpec((tm, tn), lambda i, j: (i, j)),
        out_shape=jax.ShapeDtypeStruct((m, n), F32),
        scratch_shapes=[pltpu.VMEM((tm, k), w.dtype)],
        compiler_params=_cparams(("parallel", "arbitrary")),
        name="norm_matmul" if normalize else "plain_matmul",
    )(x, g, w)


def _rmsnorm_rows_kernel(x_ref, g_ref, o_ref):
    o_ref[...] = _rms_scale(x_ref[...], g_ref[...])


def rmsnorm_rows(x, g):
    return pl.pallas_call(
        _rmsnorm_rows_kernel,
        out_shape=jax.ShapeDtypeStruct(x.shape, F32),
        name="rmsnorm_rows",
    )(x, g)


def _rwkv_prep_kernel(*refs, has_vres, width, precise):
    if has_vres:
        (zr_ref, zl_ref, pfr_ref, pfl_ref, mur_ref, mul_ref, wd_ref, wi_ref, wv_ref, pv_ref, vf_ref,
         r_o, lw_o, k_o, v_o, a_o, b_o, carry_r, carry_l) = refs
    else:
        (zr_ref, zl_ref, pfr_ref, pfl_ref, mur_ref, mul_ref, wd_ref, wi_ref, pv_ref,
         r_o, lw_o, k_o, v_o, a_o, b_o, carry_r, carry_l) = refs
        wv_ref = vf_ref = None
    tt = zr_ref.shape[1]

    @pl.when(pl.program_id(1) == 0)
    def _():
        carry_r[...] = pfr_ref[0]
        carry_l[...] = pfl_ref[0]

    if tt > 1:
        first_row = lax.broadcasted_iota(jnp.int32, (tt, 1), 0) == 0

    def shift_mix(z, carry, mu):
        if tt > 1:
            prev = jnp.where(first_row, carry, pltpu.roll(z, 1, 0))
        else:
            prev = carry
        return z + mu * (prev - z)

    zl = zl_ref[0]
    ls = shift_mix(zl, carry_l[...], mul_ref[...])
    carry_l[...] = zl[tt - 1:tt, :]

    decay_0 = pv_ref[0:1, :]
    iclr_0 = pv_ref[1:2, :]
    vres_0 = pv_ref[2:3, :]
    k_k = pv_ref[3:4, :]
    k_a = pv_ref[4:5, :]

    log_w = -_softplus(-(decay_0 + _mm(jnp.tanh(ls), wd_ref[...], precise))) - 0.5
    lw_o[0] = -jnp.exp(log_w)
    rate = _sigmoid(iclr_0 + _mm(ls, wi_ref[...], precise))

    def seg(i):
        z = zr_ref[0, :, i * width:(i + 1) * width]
        out = shift_mix(z, carry_r[:, i * width:(i + 1) * width], mur_ref[:, i * width:(i + 1) * width])
        carry_r[:, i * width:(i + 1) * width] = z[tt - 1:tt, :]
        return out

    r_o[0] = seg(0)
    k = seg(1)
    v = seg(2)
    if has_vres:
        v = v + (vf_ref[0] - v) * _sigmoid(vres_0 + _mm(ls, wv_ref[...], precise))
    v_o[0] = v

    q01 = _head_sum_matrix(LANES, A_HEAD_DIM)
    kk = k * k_k
    for p in range(width // LANES):
        sl = slice(p * LANES, (p + 1) * LANES)
        kp = kk[:, sl]
        norm = jnp.sqrt(_dot_rhs01(kp * kp, q01))
        kp = kp / jnp.maximum(norm, 1e-12)
        a_o[0, :, sl] = -kp
        b_o[0, :, sl] = kp * rate[:, sl]
    k_o[0] = k * (1.0 + (rate - 1.0) * k_a)


def rwkv_prep(z3, zprev3, mu_r, mu_l, wd, wi, wv, pvec, v_first, *, tt, width, lora_col):
    b, t, _ = z3.shape
    has_vres = wv is not None
    rkv = 3 * width
    in_specs = [
        pl.BlockSpec((1, tt, rkv), lambda i, j: (i, j, 0)),
        pl.BlockSpec((1, tt, LORA_PAD), lambda i, j: (i, j, lora_col)),
        pl.BlockSpec((1, 1, rkv), lambda i, j: (i, 0, 0)),
        pl.BlockSpec((1, 1, LORA_PAD), lambda i, j: (i, 0, lora_col)),
        pl.BlockSpec((1, rkv), lambda i, j: (0, 0)),
        pl.BlockSpec((1, LORA_PAD), lambda i, j: (0, 0)),
        pl.BlockSpec((LORA_PAD, width), lambda i, j: (0, 0)),
        pl.BlockSpec((LORA_PAD, width), lambda i, j: (0, 0)),
    ]
    args = [z3, z3, zprev3, zprev3, mu_r, mu_l, wd, wi]
    if has_vres:
        in_specs.append(pl.BlockSpec((LORA_PAD, width), lambda i, j: (0, 0)))
        args.append(wv)
    in_specs.append(pl.BlockSpec((8, width), lambda i, j: (0, 0)))
    args.append(pvec)
    if has_vres:
        in_specs.append(pl.BlockSpec((1, tt, width), lambda i, j: (i, j, 0)))
        args.append(v_first)
    out_spec = pl.BlockSpec((1, tt, width), lambda i, j: (i, j, 0))
    out_sds = jax.ShapeDtypeStruct((b, t, width), F32)
    return pl.pallas_call(
        functools.partial(_rwkv_prep_kernel, has_vres=has_vres, width=width, precise=wd.dtype == F32),
        grid=(b, t // tt),
        in_specs=in_specs,
        out_specs=[out_spec] * 6,
        out_shape=[out_sds] * 6,
        scratch_shapes=[pltpu.VMEM((1, rkv), F32), pltpu.VMEM((1, LORA_PAD), F32)],
        compiler_params=_cparams(("parallel", "arbitrary")),
        name="rwkv_prep",
    )(*args)


def _stack_heads(x, lane_is_head0):
    zero = jnp.zeros_like(x)
    return jnp.concatenate([jnp.where(lane_is_head0, x, zero), jnp.where(lane_is_head0, zero, x)], axis=0)


def _rwkv_pair_chunk(r, lw, k, v, a, b, h_bd, c):
    L = r.shape[0]
    cum = _dot_lhs01(c["tri"], lw)
    tot = cum[L - 1:L, :]
    g_in = jnp.exp(cum)
    g_ex = jnp.exp(cum - lw)
    g_inv = jnp.exp(-cum)
    g_end = jnp.exp(tot - cum)
    st = functools.partial(_stack_heads, lane_is_head0=c["head0"])
    a_st = st(a * g_ex)
    r_st = st(r * g_in)
    v_st = st(v)
    lhs = jnp.concatenate([a_st, r_st], axis=0)
    rhs = jnp.concatenate([st(b * g_inv), st(k * g_inv)], axis=0)
    gram = _bdot_nt(lhs, rhs)
    n2 = 2 * L
    zero = jnp.zeros((n2, n2), F32)
    a_ab = jnp.where(c["strict"], gram[0:n2, 0:n2], zero)
    a_ak = jnp.where(c["strict"], gram[0:n2, n2:2 * n2], zero)
    a_rb = jnp.where(c["incl"], gram[n2:2 * n2, 0:n2], zero)
    a_rk = jnp.where(c["incl"], gram[n2:2 * n2, n2:2 * n2], zero)
    pw = a_ab
    inv = c["eye"] + a_ab
    n = 1
    while 2 * n < L:
        pw = _bdot(pw, pw)
        inv = inv + _bdot(inv, pw)
        n *= 2
    a_p = _bdot(inv, a_st)
    u0 = _bdot(inv, _bdot(a_ak, v_st))
    uv = jnp.concatenate([u0, v_st], axis=0)
    bk_end = jnp.concatenate([st(b * g_end), st(k * g_end)], axis=0)
    bk_end_t = bk_end.T
    m_bd = c["eye"] * jnp.exp(tot) + _bdot(bk_end_t[:, 0:n2], a_p)
    c_bd = _bdot(bk_end_t, uv)
    r_p = r_st + _bdot(a_rb, a_p)
    y0 = _bdot(jnp.concatenate([a_rb, a_rk], axis=1), uv)
    y_st = _bdot(r_p, h_bd) + y0
    h_new = _bdot(m_bd, h_bd) + c_bd
    return y_st[0:L, :] + y_st[L:n2, :], h_new


def _rwkv_chunk_kernel(r_ref, lw_ref, k_ref, v_ref, a_ref, b_ref, pv_ref, y_ref, hout_ref, h_ref, *, pairs):
    L = r_ref.shape[1]
    n2 = 2 * L

    @pl.when(pl.program_id(2) == 0)
    def _():
        h_ref[...] = jnp.zeros_like(h_ref)

    ri = lax.broadcasted_iota(jnp.int32, (n2, n2), 0)
    ci = lax.broadcasted_iota(jnp.int32, (n2, n2), 1)
    consts = {
        "tri": (lax.broadcasted_iota(jnp.int32, (L, L), 0) >= lax.broadcasted_iota(jnp.int32, (L, L), 1)).astype(BF16),
        "head0": lax.broadcasted_iota(jnp.int32, (1, LANES), 1) < A_HEAD_DIM,
        "strict": (ri % L) > (ci % L),
        "incl": (ri % L) >= (ci % L),
        "eye": (ri == ci).astype(F32),
    }
    q01 = _head_sum_matrix(LANES, A_HEAD_DIM)
    inv_n = 1.0 / A_HEAD_DIM
    for p in range(pairs):
        sl = slice(p * LANES, (p + 1) * LANES)
        r = r_ref[0, :, sl]
        k = k_ref[0, :, sl]
        v = v_ref[0, :, sl]
        y, h_new = _rwkv_pair_chunk(r, lw_ref[0, :, sl], k, v, a_ref[0, :, sl], b_ref[0, :, sl], h_ref[p], consts)
        h_ref[p] = h_new
        r_k = pv_ref[0:1, sl]
        gn_w = pv_ref[1:2, sl]
        gn_b = pv_ref[2:3, sl]
        mean = _dot_rhs01(y, q01) * inv_n
        d = y - mean
        var = _dot_rhs01(d * d, q01) * inv_n
        bonus = _dot_rhs01(r * k * r_k, q01) * v
        y_ref[0, :, sl] = d * lax.rsqrt(var + GN_EPS) * gn_w + gn_b + bonus

    @pl.when(pl.program_id(2) == pl.num_programs(2) - 1)
    def _():
        hout_ref[0] = h_ref[...]


def rwkv_chunk_scan(r, lw, k, v, a, b, pvec, *, pairs_per_step):
    bsz, t, w = r.shape
    n_pairs = w // LANES
    assert n_pairs % pairs_per_step == 0 and t % RWKV_CHUNK == 0
    gw = pairs_per_step * LANES
    io_spec = pl.BlockSpec((1, RWKV_CHUNK, gw), lambda i, g, c: (i, c, g))
    return pl.pallas_call(
        functools.partial(_rwkv_chunk_kernel, pairs=pairs_per_step),
        grid=(bsz, n_pairs // pairs_per_step, t // RWKV_CHUNK),
        in_specs=[io_spec] * 6 + [pl.BlockSpec((8, gw), lambda i, g, c: (0, g))],
        out_specs=[io_spec, pl.BlockSpec((1, pairs_per_step, LANES, LANES), lambda i, g, c: (i, g, 0, 0))],
        out_shape=[jax.ShapeDtypeStruct((bsz, t, w), F32),
                   jax.ShapeDtypeStruct((bsz, n_pairs, LANES, LANES), F32)],
        scratch_shapes=[pltpu.VMEM((pairs_per_step, LANES, LANES), F32)],
        compiler_params=_cparams(("parallel", "parallel", "arbitrary")),
        name="rwkv_chunk_scan",
    )(r, lw, k, v, a, b, pvec)


def _rwkv_step_kernel(s_ref, r_ref, lw_ref, k_ref, a_ref, b_ref, v_ref, rk_ref, gw_ref, gb_ref, y_ref, so_ref):
    s = s_ref[0]
    r = r_ref[0]
    k = k_ref[0]
    v = v_ref[0]
    sa = jnp.sum(s * a_ref[0], axis=-1, keepdims=True)
    s_new = s * jnp.exp(lw_ref[0]) + sa * b_ref[0] + v * k
    so_ref[0] = s_new
    y = jnp.sum(s_new * r, axis=-1, keepdims=True)
    mean = jnp.mean(y, axis=1, keepdims=True)
    d = y - mean
    var = jnp.mean(d * d, axis=1, keepdims=True)
    bonus = jnp.sum(r * k * rk_ref[...], axis=-1, keepdims=True) * v
    y_ref[0] = d * lax.rsqrt(var + GN_EPS) * gw_ref[...] + gb_ref[...] + bonus


def rwkv_step(state, r, lw, k, v, a, b, r_k, gn_w, gn_b):
    bsz, h, n, _ = state.shape
    row = lambda x: x.reshape(bsz, h, 1, n)
    col = lambda x: x.reshape(bsz, h, n, 1)
    row_spec = pl.BlockSpec((1, h, 1, n), lambda i: (i, 0, 0, 0))
    col_spec = pl.BlockSpec((1, h, n, 1), lambda i: (i, 0, 0, 0))
    st_spec = pl.BlockSpec((1, h, n, n), lambda i: (i, 0, 0, 0))
    prow = pl.BlockSpec((h, 1, n), lambda i: (0, 0, 0))
    pcol = pl.BlockSpec((h, n, 1), lambda i: (0, 0, 0))
    y, s_new = pl.pallas_call(
        _rwkv_step_kernel,
        grid=(bsz,),
        in_specs=[st_spec, row_spec, row_spec, row_spec, row_spec, row_spec, col_spec, prow, pcol, pcol],
        out_specs=[col_spec, st_spec],
        out_shape=[jax.ShapeDtypeStruct((bsz, h, n, 1), F32), jax.ShapeDtypeStruct((bsz, h, n, n), F32)],
        compiler_params=_cparams(("parallel",)),
        name="rwkv_step",
    )(state, row(r), row(lw), row(k), row(a), row(b), col(v),
      r_k.reshape(h, 1, n), gn_w.reshape(h, n, 1), gn_b.reshape(h, n, 1))
    return y.reshape(bsz, 1, h * n), s_new


def _moba_prompt_kernel(slope_ref, q_ref, k_ref, v_ref, o_ref, kb_ref, vb_ref, kmean_ref, rel_ref):
    h = pl.program_id(1)
    qi = pl.program_id(2)
    blk = q_ref.shape[1]
    t = k_ref.shape[1]
    n_blk = t // blk
    d = q_ref.shape[2]
    slope = slope_ref[h]
    scale = d ** -0.5

    @pl.when(qi == 0)
    def _():
        kf = k_ref[0]
        kb_ref[...] = kf.astype(BF16)
        vb_ref[...] = v_ref[0].astype(BF16)
        kmean_ref[...] = jnp.sum(kf.reshape(n_blk, blk, d), axis=1) * (1.0 / blk)
        rel = (lax.broadcasted_iota(jnp.int32, (blk, blk), 0)
               - lax.broadcasted_iota(jnp.int32, (blk, blk), 1)).astype(F32)
        rel_ref[...] = -slope * rel

    q = q_ref[0]
    qb = q.astype(BF16)
    gate = _dot_nt_x3(q, kmean_ref[...])
    nidx = lax.broadcasted_iota(jnp.int32, (blk, n_blk), 1).astype(F32)
    neg_inf = jnp.full((blk, n_blk), -jnp.inf, F32)
    vals = jnp.where(nidx < qi.astype(F32), gate, neg_inf)
    sel_bias = jnp.full((blk, n_blk), NEG_BIG, F32)
    for _ in range(MOBA_TOPK):
        mx = jnp.max(vals, axis=1, keepdims=True)
        is_max = jnp.logical_and(vals == mx, mx > -jnp.inf)
        first = jnp.min(jnp.where(is_max, nidx, float(n_blk)), axis=1, keepdims=True)
        pick = nidx == first
        sel_bias = jnp.where(pick, 0.0, sel_bias)
        vals = jnp.where(pick, neg_inf, vals)
    sel_bias = sel_bias.astype(BF16)

    own = pl.multiple_of(qi * blk, blk)
    s = _bdot_nt(qb, kb_ref[pl.ds(own, blk), :]) * scale + rel_ref[...]
    causal = lax.broadcasted_iota(jnp.int32, (blk, blk), 1) <= lax.broadcasted_iota(jnp.int32, (blk, blk), 0)
    s = jnp.where(causal, s, NEG_BIG)
    m0 = jnp.max(s, axis=1, keepdims=True)
    p = jnp.exp(s - m0)
    l0 = jnp.sum(p, axis=1, keepdims=True)
    acc0 = jnp.dot(p.astype(BF16), vb_ref[pl.ds(own, blk), :], preferred_element_type=F32)
    blk_rows = lax.broadcasted_iota(jnp.int32, (n_blk, blk), 0)

    def body(n, carry):
        m, l, acc = carry
        start = pl.multiple_of(n * blk, blk)
        onehot = (blk_rows == n).astype(BF16)
        dist = ((qi - n) * blk).astype(F32)
        s = (_bdot_nt(qb, kb_ref[pl.ds(start, blk), :]) * scale + rel_ref[...] - slope * dist
             + jnp.dot(sel_bias, onehot, preferred_element_type=F32))
        m_new = jnp.maximum(m, jnp.max(s, axis=1, keepdims=True))
        alpha = jnp.exp(m - m_new)
        p = jnp.exp(s - m_new)
        l = alpha * l + jnp.sum(p, axis=1, keepdims=True)
        acc = alpha * acc + jnp.dot(p.astype(BF16), vb_ref[pl.ds(start, blk), :], preferred_element_type=F32)
        return m_new, l, acc

    _, l, acc = lax.fori_loop(0, qi, body, (m0, l0, acc0))
    o_ref[0] = acc / l


def moba_prompt(z3, slopes, *, heads):
    bsz, t, _ = z3.shape
    d = B_HEAD_DIM
    blk = MOBA_BLOCK
    assert t % blk == 0
    return pl.pallas_call(
        _moba_prompt_kernel,
        grid_spec=pltpu.PrefetchScalarGridSpec(
            num_scalar_prefetch=0,
            grid=(bsz, heads, t // blk),
            in_specs=[pl.BlockSpec(memory_space=pltpu.SMEM),
                      pl.BlockSpec((1, blk, d), lambda b, h, i: (b, i, h)),
                      pl.BlockSpec((1, t, d), lambda b, h, i: (b, 0, heads + h)),
                      pl.BlockSpec((1, t, d), lambda b, h, i: (b, 0, 2 * heads + h))],
            out_specs=pl.BlockSpec((1, blk, d), lambda b, h, i: (b, i, h)),
            scratch_shapes=[pltpu.VMEM((t, d), BF16), pltpu.VMEM((t, d), BF16),
                            pltpu.VMEM((t // blk, d), F32), pltpu.VMEM((blk, blk), F32)],
        ),
        out_shape=jax.ShapeDtypeStruct((bsz, t, heads * d), F32),
        compiler_params=_cparams(("parallel", "parallel", "arbitrary")),
        name="moba_prompt",
    )(slopes, z3, z3, z3)


def _paged_block_sum_kernel(pt_ref, k0_ref, k1_ref, o_ref):
    o_ref[0, 0] = jnp.sum(k0_ref[...], axis=0) + jnp.sum(k1_ref[...], axis=0)


def paged_block_sums(cache_k, page_table, *, pages_per_block):
    assert pages_per_block == 2
    _, page, h, d = cache_k.shape
    bsz, n_pages = page_table.shape
    n_blk = n_pages // pages_per_block
    return pl.pallas_call(
        _paged_block_sum_kernel,
        grid_spec=pltpu.PrefetchScalarGridSpec(
            num_scalar_prefetch=1,
            grid=(bsz, n_blk),
            in_specs=[pl.BlockSpec((None, page, h, d), lambda b, n, pt: (pt[b, 2 * n], 0, 0, 0)),
                      pl.BlockSpec((None, page, h, d), lambda b, n, pt: (pt[b, 2 * n + 1], 0, 0, 0))],
            out_specs=pl.BlockSpec((1, 1, h, d), lambda b, n, pt: (b, n, 0, 0)),
        ),
        out_shape=jax.ShapeDtypeStruct((bsz, n_blk, h, d), F32),
        compiler_params=_cparams(("parallel", "arbitrary")),
        name="paged_block_sums",
    )(page_table, cache_k, cache_k)


def _moba_select_kernel(ks_ref, q_ref, o_ref, *, block_len):
    n_blk, h = ks_ref.shape[1], ks_ref.shape[2]
    kmean = ks_ref[0] * (1.0 / block_len)
    gate = jnp.sum(kmean * q_ref[0], axis=-1, keepdims=True)
    nidx = lax.broadcasted_iota(jnp.int32, (n_blk, h, 1), 0).astype(F32)
    vals = gate
    for t in range(MOBA_TOPK):
        mx = jnp.max(vals, axis=0, keepdims=True)
        first = jnp.min(jnp.where(vals == mx, nidx, float(n_blk)), axis=0, keepdims=True)
        o_ref[0, t] = first[0].astype(jnp.int32)
        vals = jnp.where(nidx == first, -jnp.inf, vals)


def moba_select(ksum, q, *, block_len):
    bsz, n_blk, h, d = ksum.shape
    return pl.pallas_call(
        functools.partial(_moba_select_kernel, block_len=block_len),
        grid=(bsz,),
        in_specs=[pl.BlockSpec((1, n_blk, h, d), lambda b: (b, 0, 0, 0)),
                  pl.BlockSpec((1, h, d), lambda b: (b, 0, 0))],
        out_specs=pl.BlockSpec((1, MOBA_TOPK, h, 1), lambda b: (b, 0, 0, 0)),
        out_shape=jax.ShapeDtypeStruct((bsz, MOBA_TOPK, h, 1), jnp.int32),
        compiler_params=_cparams(("parallel",)),
        name="moba_select",
    )(ksum, q)


def _moba_decode_kernel(pt_ref, sel_ref, slope_ref, q_ref, kn_ref, vn_ref, ck_ref, cv_ref, o_ref,
                        kbuf, vbuf, sems, *, past_len, page, heads):
    b = pl.program_id(0)
    d = B_HEAD_DIM
    scale = d ** -0.5
    pages_per_block = MOBA_BLOCK // page
    n_slots = MOBA_TOPK * pages_per_block

    def page_copies(h, s):
        blk = sel_ref[b, s // pages_per_block, h]
        pg = pt_ref[b, blk * pages_per_block + s % pages_per_block]
        return (pltpu.make_async_copy(ck_ref.at[pg, :, h, :], kbuf.at[h, s], sems.at[0, h]),
                pltpu.make_async_copy(cv_ref.at[pg, :, h, :], vbuf.at[h, s], sems.at[1, h]))

    for h in range(heads):
        for s in range(n_slots):
            for cp in page_copies(h, s):
                cp.start()

    lane = lax.broadcasted_iota(jnp.int32, (1, page), 1)
    for h in range(heads):
        for s in range(n_slots):
            for cp in page_copies(h, s):
                cp.wait()
        sl = slice(h * d, (h + 1) * d)
        slope = slope_ref[h]
        q8 = jnp.broadcast_to(q_ref[0, :, sl], (8, d))
        logits = []
        for s in range(n_slots):
            blk = sel_ref[b, s // pages_per_block, h]
            kpos = blk * MOBA_BLOCK + (s % pages_per_block) * page + lane
            sc = _mm_nt(q8, kbuf[h, s], True)[0:1, :] * scale
            logits.append(sc - slope * (past_len - kpos).astype(F32))
        own = jnp.sum(q8[0:1, :] * kn_ref[0, :, sl], axis=1, keepdims=True) * scale
        m = own
        for sc in logits:
            m = jnp.maximum(m, jnp.max(sc, axis=1, keepdims=True))
        p_own = jnp.exp(own - m)
        l = p_own
        acc = p_own * vn_ref[0, :, sl]
        for s in range(n_slots):
            p = jnp.exp(logits[s] - m)
            l = l + jnp.sum(p, axis=1, keepdims=True)
            acc = acc + _mm(jnp.broadcast_to(p, (8, page)), vbuf[h, s], True)[0:1, :]
        o_ref[0, :, sl] = acc / l


def moba_decode(z3, cache_k, cache_v, page_table, sel, slopes, *, past_len):
    bsz = z3.shape[0]
    _, page, heads, d = cache_k.shape
    w = heads * d
    n_slots = MOBA_TOPK * (MOBA_BLOCK // page)
    row_spec = lambda col: pl.BlockSpec((1, 1, w), lambda b, pt, sel_: (b, 0, col))
    return pl.pallas_call(
        functools.partial(_moba_decode_kernel, past_len=past_len, page=page, heads=heads),
        grid_spec=pltpu.PrefetchScalarGridSpec(
            num_scalar_prefetch=2,
            grid=(bsz,),
            in_specs=[pl.BlockSpec(memory_space=pltpu.SMEM), row_spec(0), row_spec(1), row_spec(2),
                      pl.BlockSpec(memory_space=pl.ANY), pl.BlockSpec(memory_space=pl.ANY)],
            out_specs=pl.BlockSpec((1, 1, w), lambda b, pt, sel_: (b, 0, 0)),
            scratch_shapes=[pltpu.VMEM((heads, n_slots, page, d), F32),
                            pltpu.VMEM((heads, n_slots, page, d), F32),
                            pltpu.SemaphoreType.DMA((2, heads))],
        ),
        out_shape=jax.ShapeDtypeStruct((bsz, 1, w), F32),
        compiler_params=_cparams(("arbitrary",)),
        name="moba_decode",
    )(page_table, sel, slopes, z3, z3, z3, cache_k, cache_v)


def _mem_attn_kernel(q_ref, k_ref, v_ref, o_ref, *, precise):
    tq = q_ref.shape[1]
    scale = C_HEAD_DIM ** -0.5
    rows = max(tq, 8)
    for h in range(C_HEADS):
        sl = slice(h * C_HEAD_DIM, (h + 1) * C_HEAD_DIM)
        q = jnp.broadcast_to(q_ref[0, :, sl], (rows, C_HEAD_DIM)) if tq < 8 else q_ref[0, :, sl]
        s = _mm_nt(q, k_ref[0, :, sl], precise) * scale
        m = jnp.max(s, axis=1, keepdims=True)
        p = jnp.exp(s - m)
        l = jnp.sum(p, axis=1, keepdims=True)
        o = _mm(p, v_ref[0, :, sl], precise) / l
        o_ref[0, :, sl] = o[0:tq, :]


def mem_attention(z3, mem_k3, mem_v3, *, q_col, k_col, v_col, tq, precise):
    bsz, t, _ = z3.shape
    mlen = mem_k3.shape[1]
    return pl.pallas_call(
        functools.partial(_mem_attn_kernel, precise=precise),
        grid=(bsz, t // tq),
        in_specs=[pl.BlockSpec((1, tq, C_WIDTH), lambda b, i: (b, i, q_col)),
                  pl.BlockSpec((1, mlen, C_WIDTH), lambda b, i: (b, 0, k_col)),
                  pl.BlockSpec((1, mlen, C_WIDTH), lambda b, i: (b, 0, v_col))],
        out_specs=pl.BlockSpec((1, tq, C_WIDTH), lambda b, i: (b, i, 0)),
        out_shape=jax.ShapeDtypeStruct((bsz, t, C_WIDTH), F32),
        compiler_params=_cparams(("parallel", "parallel")),
        name="mem_attention",
    )(z3, mem_k3, mem_v3)


def _out_proj_kernel(ymix_ref, ymem_ref, g0_ref, g1_ref, w_ref, g_ref, x_ref, o_ref, *, precise):
    half = g0_ref.shape[1]
    wmix = ymix_ref.shape[1]
    silu = lambda t: t * _sigmoid(t)
    g1 = g1_ref[...]
    u0 = ymix_ref[:, 0:half] * silu(g0_ref[...])
    u1 = ymix_ref[:, half:wmix] * silu(g1[:, 0:wmix - half])
    u2 = ymem_ref[...] * silu(g1[:, wmix - half:])
    y = _mm(u0, w_ref[0:half, :], precise)
    y = y + _mm(u1, w_ref[half:wmix, :], precise)
    y = y + _mm(u2, w_ref[wmix:, :], precise)
    o_ref[...] = x_ref[...] + _rms_scale(y, g_ref[...])


def out_proj(ymix, ymem, z, w_out, g, x, *, gate_col, tm):
    m, dm = x.shape
    wmix = ymix.shape[1]
    half = 1024
    return pl.pallas_call(
        functools.partial(_out_proj_kernel, precise=w_out.dtype == F32),
        grid=(m // tm,),
        in_specs=[pl.BlockSpec((tm, wmix), lambda i: (i, 0)),
                  pl.BlockSpec((tm, C_WIDTH), lambda i: (i, 0)),
                  pl.BlockSpec((tm, half), lambda i: (i, gate_col)),
                  pl.BlockSpec((tm, half), lambda i: (i, gate_col + 1)),
                  pl.BlockSpec(w_out.shape, lambda i: (0, 0)),
                  pl.BlockSpec((1, dm), lambda i: (0, 0)),
                  pl.BlockSpec((tm, dm), lambda i: (i, 0))],
        out_specs=pl.BlockSpec((tm, dm), lambda i: (i, 0)),
        out_shape=jax.ShapeDtypeStruct((m, dm), F32),
        compiler_params=_cparams(("parallel",)),
        name="out_proj",
    )(ymix, ymem, z, z, w_out, g, x)


def _prep_rwkv_weights(w_in, mu, decay_b, iclr_b, vres_b, a_width):
    rkv = 3 * a_width
    n_lora = mu.shape[0] - rkv
    pad = LORA_PAD - n_lora
    dm = w_in.shape[0]
    w = jnp.concatenate([w_in[:, :rkv], w_in[:, rkv + n_lora:], w_in[:, rkv:rkv + n_lora],
                         jnp.zeros((dm, pad), w_in.dtype)], axis=1)
    mu_r = mu[:rkv].reshape(1, rkv)
    mu_l = jnp.pad(mu[rkv:], (0, pad)).reshape(1, LORA_PAD)

    def pad_rows(m, start):
        return jnp.pad(m, ((start, LORA_PAD - start - m.shape[0]), (0, 0)))

    wd = pad_rows(decay_b, 0)
    wi = pad_rows(iclr_b, DECAY_LORA)
    wv = None if vres_b is None else pad_rows(vres_b, DECAY_LORA + ICLR_LORA)
    return w, mu_r, mu_l, wd, wi, wv


def kernel(x_prompt, x_sample, state_wkv, state_shift, cache_k_l1, cache_v_l1, cache_k_l3, cache_v_l3, cache_mem_k, cache_mem_v, page_table, mem_prompt, norm_pre, norm_post, w_in_l0, w_in_l1, w_in_l2, w_in_l3, w_mem_kv, w_out, mu_l0, mu_l2, rwkv_decay_b, rwkv_decay_0, rwkv_iclr_b, rwkv_iclr_0, rwkv_k_k, rwkv_k_a, rwkv_r_k, rwkv_gn_w, rwkv_gn_b, vres_b_l2, vres_0_l2):
    depth = norm_pre.shape[0]
    bp, seq, dm = x_prompt.shape
    bs, dec_seq, _ = x_sample.shape
    assert dec_seq == 1
    a_heads = state_wkv.shape[2]
    a_width = a_heads * A_HEAD_DIM
    n_pool, page, b_heads, _ = cache_k_l1.shape
    b_width = b_heads * B_HEAD_DIM
    mem_len = mem_prompt.shape[1]
    past_len = page_table.shape[1] * page
    assert past_len % MOBA_BLOCK == 0 and past_len // MOBA_BLOCK >= MOBA_TOPK and MOBA_BLOCK % page == 0
    assert a_width == b_width
    mix_w = a_width
    qmem_col = 3 * mix_w // C_WIDTH
    gate_col = (3 * mix_w + C_WIDTH) // 1024
    lora_col = (3 * mix_w + C_WIDTH + dm) // LORA_PAD

    zeros8 = jnp.zeros((a_width,), F32)
    rw = []
    for j, (w_in, mu, vb, v0) in enumerate(((w_in_l0, mu_l0, None, None), (w_in_l2, mu_l2, vres_b_l2, vres_0_l2))):
        w, mu_r, mu_l, wd, wi, wv = _prep_rwkv_weights(w_in, mu, rwkv_decay_b[j], rwkv_iclr_b[j], vb, a_width)
        pvec = jnp.stack([rwkv_decay_0[j], rwkv_iclr_0[j], zeros8 if v0 is None else v0, rwkv_k_k[j], rwkv_k_a[j],
                          zeros8, zeros8, zeros8])
        pvec2 = jnp.stack([rwkv_r_k[j].reshape(-1), rwkv_gn_w[j], rwkv_gn_b[j]] + [zeros8] * 5)
        lora = {True: (wd.astype(BF16), wi.astype(BF16), None if wv is None else wv.astype(BF16)),
                False: (wd, wi, wv)}
        rw.append(dict(w={True: w.astype(BF16), False: w}, lora=lora, mu_r=mu_r, mu_l=mu_l, pvec=pvec, pvec2=pvec2))
    w_moba = [{True: w.astype(BF16), False: w} for w in (w_in_l1, w_in_l3)]
    w_out_p = {True: w_out.astype(BF16), False: w_out}
    w_mem_all = jnp.transpose(w_mem_kv, (1, 0, 2)).reshape(dm, depth * 2 * C_WIDTH).astype(BF16)
    slopes = jnp.asarray(alibi_slopes(b_heads))
    ones_g = jnp.ones((1, dm), F32)
    caches = [(cache_k_l1, cache_v_l1), (cache_k_l3, cache_v_l3)]

    mem_rows = mem_prompt.reshape(bp * mem_len, dm)
    memkv = norm_matmul(mem_rows, ones_g, w_mem_all, normalize=False, tm=bp * mem_len, tn=512)
    memkv3 = memkv.reshape(bp, mem_len, depth * 2 * C_WIDTH)
    memkv5 = memkv.reshape(bp, mem_len, depth, 2, C_HEADS, C_HEAD_DIM)
    p_mem_k = jnp.transpose(memkv5[:, :, :, 0], (2, 0, 1, 3, 4))
    p_mem_v = jnp.transpose(memkv5[:, :, :, 1], (2, 0, 1, 3, 4))
    cache_mem_k3 = cache_mem_k.reshape(depth, bs, mem_len, C_WIDTH)
    cache_mem_v3 = cache_mem_v.reshape(depth, bs, mem_len, C_WIDTH)

    def trunk(x3, is_prompt):
        bsz, t, _ = x3.shape
        m = bsz * t
        x = x3.reshape(m, dm)
        tm_a = 1024 if m % 1024 == 0 else m
        tm_f = 256 if m % 256 == 0 else m
        v_first = None
        wkv_out, shift_out, kv_out = [], [], []
        for i in range(depth):
            j = i // 2
            g_pre = norm_pre[i].reshape(1, dm)
            if i % 2 == 0:
                p = rw[j]
                z = norm_matmul(x, g_pre, p["w"][is_prompt], normalize=True, tm=tm_a, tn=512)
                z3 = z.reshape(bsz, t, -1)
                if is_prompt:
                    zprev3 = jnp.zeros((bsz, 1, z.shape[1]), F32)
                else:
                    zprev3 = norm_matmul(state_shift[j], ones_g, p["w"][is_prompt], normalize=False, tm=bsz, tn=512
                                         ).reshape(bsz, 1, -1)
                wd, wi, wv = p["lora"][is_prompt]
                r, lw, k, v, a, b = rwkv_prep(z3, zprev3, p["mu_r"], p["mu_l"], wd, wi, wv, p["pvec"],
                                              v_first, tt=min(t, 128), width=a_width, lora_col=lora_col)
                if v_first is None:
                    v_first = v
                if is_prompt:
                    y_mix, h_bd = rwkv_chunk_scan(r, lw, k, v, a, b, p["pvec2"], pairs_per_step=4)
                    hb = h_bd.reshape(bsz, a_heads // 2, 2, A_HEAD_DIM, 2, A_HEAD_DIM)
                    s_new = jnp.stack([hb[:, :, 0, :, 0, :], hb[:, :, 1, :, 1, :]], axis=2)
                    s_new = jnp.swapaxes(s_new.reshape(bsz, a_heads, A_HEAD_DIM, A_HEAD_DIM), -1, -2)
                else:
                    y_mix, s_new = rwkv_step(state_wkv[j], r, lw, k, v, a, b, rwkv_r_k[j], rwkv_gn_w[j], rwkv_gn_b[j])
                wkv_out.append(s_new)
                shift_out.append(rmsnorm_rows(x3[:, -1], g_pre))
            else:
                z = norm_matmul(x, g_pre, w_moba[j][is_prompt], normalize=True, tm=tm_a, tn=512)
                z3 = z.reshape(bsz, t, -1)
                if is_prompt:
                    y_mix = moba_prompt(z3, slopes, heads=b_heads)
                else:
                    ck, cv = caches[j]
                    ksum = paged_block_sums(ck, page_table, pages_per_block=MOBA_BLOCK // page)
                    sel = moba_select(ksum, z3[:, 0, :b_width].reshape(bsz, b_heads, B_HEAD_DIM),
                                      block_len=MOBA_BLOCK)
                    y_mix = moba_decode(z3, ck, cv, page_table, sel[..., 0], slopes, past_len=past_len)
                kv_out.append((z3[:, :, b_width:2 * b_width].reshape(bsz, t, b_heads, B_HEAD_DIM),
                               z3[:, :, 2 * b_width:3 * b_width].reshape(bsz, t, b_heads, B_HEAD_DIM)))
            if is_prompt:
                y_mem = mem_attention(z3, memkv3, memkv3, q_col=qmem_col, k_col=2 * i, v_col=2 * i + 1,
                                      tq=min(t, 512), precise=False)
            else:
                y_mem = mem_attention(z3, cache_mem_k3[i], cache_mem_v3[i], q_col=qmem_col, k_col=0, v_col=0, tq=t,
                                      precise=True)
            x = out_proj(y_mix.reshape(m, mix_w), y_mem.reshape(m, C_WIDTH), z, w_out_p[is_prompt][i],
                         norm_post[i].reshape(1, dm), x, gate_col=gate_col, tm=tm_f)
            x3 = x.reshape(bsz, t, dm)
        return x3, jnp.stack(wkv_out), jnp.stack(shift_out), kv_out

    y_prompt, p_wkv, p_shift, p_kv = trunk(x_prompt, True)
    y_sample, s_wkv, s_shift, s_kv = trunk(x_sample, False)
    (p_k_l1, p_v_l1), (p_k_l3, p_v_l3) = p_kv
    (s_k_l1, s_v_l1), (s_k_l3, s_v_l3) = s_kv
    return (y_prompt, y_sample, p_wkv, p_shift, p_k_l1, p_v_l1, p_k_l3, p_v_l3, p_mem_k, p_mem_v,
            s_wkv, s_shift, s_k_l1, s_v_l1, s_k_l3, s_v_l3)
```

```python
import functools
import math

import numpy as np
import jax
import jax.numpy as jnp
from jax import lax
from jax.experimental import pallas as pl
from jax.experimental.pallas import tpu as pltpu

F32 = jnp.float32
BF16 = jnp.bfloat16

RMS_EPS = 1e-6
GN_EPS = 64e-5
A_HEAD_DIM = 64
B_HEAD_DIM = 128
C_HEADS = 4
C_HEAD_DIM = 128
C_WIDTH = C_HEADS * C_HEAD_DIM
MOBA_BLOCK = 256
MOBA_TOPK = 3
DECAY_LORA = 96
ICLR_LORA = 96
VRES_LORA = 64

LANES = 128
RWKV_CHUNK = 64
LORA_PAD = 512
NEG_BIG = -1e30
VMEM_LIMIT = 48 * 1024 * 1024


def _cparams(sem):
    return pltpu.CompilerParams(dimension_semantics=sem, vmem_limit_bytes=VMEM_LIMIT)


def alibi_slopes(n):
    def pow2_slopes(m):
        start = 2.0 ** (-(2.0 ** -(math.log2(m) - 3)))
        return [start ** (i + 1) for i in range(m)]
    if math.log2(n).is_integer():
        s = pow2_slopes(n)
    else:
        c = 2 ** math.floor(math.log2(n))
        s = pow2_slopes(c) + pow2_slopes(2 * c)[0::2][: n - c]
    return np.asarray(s, dtype=np.float32)


def _bdot(a, b):
    return jnp.dot(a.astype(BF16), b.astype(BF16), preferred_element_type=F32)


def _bdot_nt(a, b):
    return lax.dot_general(a.astype(BF16), b.astype(BF16), (((1,), (1,)), ((), ())),
                           preferred_element_type=F32)


def _mm(a, b, precise):
    if precise:
        return jnp.dot(a, b, preferred_element_type=F32, precision=lax.Precision.HIGHEST)
    return _bdot(a, b)


def _mm_nt(a, b, precise):
    if precise:
        return lax.dot_general(a, b, (((1,), (1,)), ((), ())), preferred_element_type=F32,
                               precision=lax.Precision.HIGHEST)
    return _bdot_nt(a, b)


def _split3(x):
    hi = x.astype(BF16)
    r1 = x - hi.astype(F32)
    mid = r1.astype(BF16)
    lo = (r1 - mid.astype(F32)).astype(BF16)
    return hi, mid, lo


def _dot_rhs01(x, q01):
    hi, mid, lo = _split3(x)
    acc = jnp.dot(lo, q01, preferred_element_type=F32)
    acc = acc + jnp.dot(mid, q01, preferred_element_type=F32)
    return acc + jnp.dot(hi, q01, preferred_element_type=F32)


def _dot_lhs01(t01, x):
    hi, mid, lo = _split3(x)
    acc = jnp.dot(t01, lo, preferred_element_type=F32)
    acc = acc + jnp.dot(t01, mid, preferred_element_type=F32)
    return acc + jnp.dot(t01, hi, preferred_element_type=F32)


def _dot_nt_x3(a, b):
    ah = a.astype(BF16)
    al = (a - ah.astype(F32)).astype(BF16)
    bh = b.astype(BF16)
    bl = (b - bh.astype(F32)).astype(BF16)
    dn = (((1,), (1,)), ((), ()))
    acc = lax.dot_general(al, bh, dn, preferred_element_type=F32)
    acc = acc + lax.dot_general(ah, bl, dn, preferred_element_type=F32)
    return acc + lax.dot_general(ah, bh, dn, preferred_element_type=F32)


def _head_sum_matrix(width, head):
    r = lax.broadcasted_iota(jnp.int32, (width, width), 0) // head
    c = lax.broadcasted_iota(jnp.int32, (width, width), 1) // head
    return (r == c).astype(BF16)


def _sigmoid(x):
    return 1.0 / (1.0 + jnp.exp(-x))


def _softplus(x):
    return jnp.maximum(x, 0.0) + jnp.log(1.0 + jnp.exp(-jnp.abs(x)))


def _rms_scale(x, g):
    ms = jnp.mean(x * x, axis=-1, keepdims=True)
    return x * lax.rsqrt(ms + RMS_EPS) * g


def _norm_matmul_kernel(x_ref, g_ref, w_ref, o_ref, h_ref, *, normalize, row_block, precise):
    @pl.when(pl.program_id(1) == 0)
    def _():
        tm = x_ref.shape[0]
        for s in range(0, tm, row_block):
            x = x_ref[s:s + row_block, :]
            if normalize:
                x = _rms_scale(x, g_ref[...])
            h_ref[s:s + row_block, :] = x.astype(h_ref.dtype)
    o_ref[...] = _mm(h_ref[...], w_ref[...], precise)


def norm_matmul(x, g, w, *, normalize, tm, tn):
    m, k = x.shape
    n = w.shape[1]
    assert m % tm == 0 and n % tn == 0, (m, tm, n, tn)
    row_block = min(tm, 256)
    precise = w.dtype == F32
    return pl.pallas_call(
        functools.partial(_norm_matmul_kernel, normalize=normalize, row_block=row_block, precise=precise),
        grid=(m // tm, n // tn),
        in_specs=[pl.BlockSpec((tm, k), lambda i, j: (i, 0)),
                  pl.BlockSpec((1, k), lambda i, j: (0, 0)),
                  pl.BlockSpec((k, tn), lambda i, j: (0, j))],
        out_specs=pl.BlockSpec((tm, tn), lambda i, j: (i, j)),
        out_shape=jax.ShapeDtypeStruct((m, n), F32),
        scratch_shapes=[pltpu.VMEM((tm, k), w.dtype)],
        compiler_params=_cparams(("parallel", "arbitrary")),
        name="norm_matmul" if normalize else "plain_matmul",
    )(x, g, w)


def _rmsnorm_rows_kernel(x_ref, g_ref, o_ref):
    o_ref[...] = _rms_scale(x_ref[...], g_ref[...])


def rmsnorm_rows(x, g):
    return pl.pallas_call(
        _rmsnorm_rows_kernel,
        out_shape=jax.ShapeDtypeStruct(x.shape, F32),
        name="rmsnorm_rows",
    )(x, g)


def _rwkv_prep_kernel(*refs, has_vres, width, precise):
    if has_vres:
        (zr_ref, zl_ref, pfr_ref, pfl_ref, mur_ref, mul_ref, wd_ref, wi_ref, wv_ref, pv_ref, vf_ref,
         r_o, lw_o, k_o, v_o, a_o, b_o, carry_r, carry_l) = refs
    else:
        (zr_ref, zl_ref, pfr_ref, pfl_ref, mur_ref, mul_ref, wd_ref, wi_ref, pv_ref,
         r_o, lw_o, k_o, v_o, a_o, b_o, carry_r, carry_l) = refs
        wv_ref = vf_ref = None
    tt = zr_ref.shape[1]

    @pl.when(pl.program_id(1) == 0)
    def _():
        carry_r[...] = pfr_ref[0]
        carry_l[...] = pfl_ref[0]

    if tt > 1:
        first_row = lax.broadcasted_iota(jnp.int32, (tt, 1), 0) == 0

    def shift_mix(z, carry, mu):
        if tt > 1:
            prev = jnp.where(first_row, carry, pltpu.roll(z, 1, 0))
        else:
            prev = carry
        return z + mu * (prev - z)

    zl = zl_ref[0]
    ls = shift_mix(zl, carry_l[...], mul_ref[...])
    carry_l[...] = zl[tt - 1:tt, :]

    decay_0 = pv_ref[0:1, :]
    iclr_0 = pv_ref[1:2, :]
    vres_0 = pv_ref[2:3, :]
    k_k = pv_ref[3:4, :]
    k_a = pv_ref[4:5, :]

    log_w = -_softplus(-(decay_0 + _mm(jnp.tanh(ls), wd_ref[...], precise))) - 0.5
    lw_o[0] = -jnp.exp(log_w)
    rate = _sigmoid(iclr_0 + _mm(ls, wi_ref[...], precise))

    def seg(i):
        z = zr_ref[0, :, i * width:(i + 1) * width]
        out = shift_mix(z, carry_r[:, i * width:(i + 1) * width], mur_ref[:, i * width:(i + 1) * width])
        carry_r[:, i * width:(i + 1) * width] = z[tt - 1:tt, :]
        return out

    r_o[0] = seg(0)
    k = seg(1)
    v = seg(2)
    if has_vres:
        v = v + (vf_ref[0] - v) * _sigmoid(vres_0 + _mm(ls, wv_ref[...], precise))
    v_o[0] = v

    q01 = _head_sum_matrix(LANES, A_HEAD_DIM)
    kk = k * k_k
    for p in range(width // LANES):
        sl = slice(p * LANES, (p + 1) * LANES)
        kp = kk[:, sl]
        norm = jnp.sqrt(_dot_rhs01(kp * kp, q01))
        kp = kp / jnp.maximum(norm, 1e-12)
        a_o[0, :, sl] = -kp
        b_o[0, :, sl] = kp * rate[:, sl]
    k_o[0] = k * (1.0 + (rate - 1.0) * k_a)


def rwkv_prep(z3, zprev3, mu_r, mu_l, wd, wi, wv, pvec, v_first, *, tt, width, lora_col):
    b, t, _ = z3.shape
    has_vres = wv is not None
    rkv = 3 * width
    in_specs = [
        pl.BlockSpec((1, tt, rkv), lambda i, j: (i, j, 0)),
        pl.BlockSpec((1, tt, LORA_PAD), lambda i, j: (i, j, lora_col)),
        pl.BlockSpec((1, 1, rkv), lambda i, j: (i, 0, 0)),
        pl.BlockSpec((1, 1, LORA_PAD), lambda i, j: (i, 0, lora_col)),
        pl.BlockSpec((1, rkv), lambda i, j: (0, 0)),
        pl.BlockSpec((1, LORA_PAD), lambda i, j: (0, 0)),
        pl.BlockSpec((LORA_PAD, width), lambda i, j: (0, 0)),
        pl.BlockSpec((LORA_PAD, width), lambda i, j: (0, 0)),
    ]
    args = [z3, z3, zprev3, zprev3, mu_r, mu_l, wd, wi]
    if has_vres:
        in_specs.append(pl.BlockSpec((LORA_PAD, width), lambda i, j: (0, 0)))
        args.append(wv)
    in_specs.append(pl.BlockSpec((8, width), lambda i, j: (0, 0)))
    args.append(pvec)
    if has_vres:
        in_specs.append(pl.BlockSpec((1, tt, width), lambda i, j: (i, j, 0)))
        args.append(v_first)
    out_spec = pl.BlockSpec((1, tt, width), lambda i, j: (i, j, 0))
    out_sds = jax.ShapeDtypeStruct((b, t, width), F32)
    return pl.pallas_call(
        functools.partial(_rwkv_prep_kernel, has_vres=has_vres, width=width, precise=wd.dtype == F32),
        grid=(b, t // tt),
        in_specs=in_specs,
        out_specs=[out_spec] * 6,
        out_shape=[out_sds] * 6,
        scratch_shapes=[pltpu.VMEM((1, rkv), F32), pltpu.VMEM((1, LORA_PAD), F32)],
        compiler_params=_cparams(("parallel", "arbitrary")),
        name="rwkv_prep",
    )(*args)


def _stack_heads(x, lane_is_head0):
    zero = jnp.zeros_like(x)
    return jnp.concatenate([jnp.where(lane_is_head0, x, zero), jnp.where(lane_is_head0, zero, x)], axis=0)


def _rwkv_pair_chunk(r, lw, k, v, a, b, h_bd, c):
    L = r.shape[0]
    cum = _dot_lhs01(c["tri"], lw)
    tot = cum[L - 1:L, :]
    g_in = jnp.exp(cum)
    g_ex = jnp.exp(cum - lw)
    g_inv = jnp.exp(-cum)
    g_end = jnp.exp(tot - cum)
    st = functools.partial(_stack_heads, lane_is_head0=c["head0"])
    a_st = st(a * g_ex)
    r_st = st(r * g_in)
    v_st = st(v)
    lhs = jnp.concatenate([a_st, r_st], axis=0)
    rhs = jnp.concatenate([st(b * g_inv), st(k * g_inv)], axis=0)
    gram = _bdot_nt(lhs, rhs)
    n2 = 2 * L
    zero = jnp.zeros((n2, n2), F32)
    a_ab = jnp.where(c["strict"], gram[0:n2, 0:n2], zero)
    a_ak = jnp.where(c["strict"], gram[0:n2, n2:2 * n2], zero)
    a_rb = jnp.where(c["incl"], gram[n2:2 * n2, 0:n2], zero)
    a_rk = jnp.where(c["incl"], gram[n2:2 * n2, n2:2 * n2], zero)
    pw = a_ab
    inv = c["eye"] + a_ab
    n = 1
    while 2 * n < L:
        pw = _bdot(pw, pw)
        inv = inv + _bdot(inv, pw)
        n *= 2
    a_p = _bdot(inv, a_st)
    u0 = _bdot(inv, _bdot(a_ak, v_st))
    uv = jnp.concatenate([u0, v_st], axis=0)
    bk_end = jnp.concatenate([st(b * g_end), st(k * g_end)], axis=0)
    bk_end_t = bk_end.T
    m_bd = c["eye"] * jnp.exp(tot) + _bdot(bk_end_t[:, 0:n2], a_p)
    c_bd = _bdot(bk_end_t, uv)
    r_p = r_st + _bdot(a_rb, a_p)
    y0 = _bdot(jnp.concatenate([a_rb, a_rk], axis=1), uv)
    y_st = _bdot(r_p, h_bd) + y0
    h_new = _bdot(m_bd, h_bd) + c_bd
    return y_st[0:L, :] + y_st[L:n2, :], h_new


def _rwkv_chunk_kernel(r_ref, lw_ref, k_ref, v_ref, a_ref, b_ref, pv_ref, y_ref, hout_ref, h_ref, *, pairs):
    L = r_ref.shape[1]
    n2 = 2 * L

    @pl.when(pl.program_id(2) == 0)
    def _():
        h_ref[...] = jnp.zeros_like(h_ref)

    ri = lax.broadcasted_iota(jnp.int32, (n2, n2), 0)
    ci = lax.broadcasted_iota(jnp.int32, (n2, n2), 1)
    consts = {
        "tri": (lax.broadcasted_iota(jnp.int32, (L, L), 0) >= lax.broadcasted_iota(jnp.int32, (L, L), 1)).astype(BF16),
        "head0": lax.broadcasted_iota(jnp.int32, (1, LANES), 1) < A_HEAD_DIM,
        "strict": (ri % L) > (ci % L),
        "incl": (ri % L) >= (ci % L),
        "eye": (ri == ci).astype(F32),
    }
    q01 = _head_sum_matrix(LANES, A_HEAD_DIM)
    inv_n = 1.0 / A_HEAD_DIM
    for p in range(pairs):
        sl = slice(p * LANES, (p + 1) * LANES)
        r = r_ref[0, :, sl]
        k = k_ref[0, :, sl]
        v = v_ref[0, :, sl]
        y, h_new = _rwkv_pair_chunk(r, lw_ref[0, :, sl], k, v, a_ref[0, :, sl], b_ref[0, :, sl], h_ref[p], consts)
        h_ref[p] = h_new
        r_k = pv_ref[0:1, sl]
        gn_w = pv_ref[1:2, sl]
        gn_b = pv_ref[2:3, sl]
        mean = _dot_rhs01(y, q01) * inv_n
        d = y - mean
        var = _dot_rhs01(d * d, q01) * inv_n
        bonus = _dot_rhs01(r * k * r_k, q01) * v
        y_ref[0, :, sl] = d * lax.rsqrt(var + GN_EPS) * gn_w + gn_b + bonus

    @pl.when(pl.program_id(2) == pl.num_programs(2) - 1)
    def _():
        hout_ref[0] = h_ref[...]


def rwkv_chunk_scan(r, lw, k, v, a, b, pvec, *, pairs_per_step):
    bsz, t, w = r.shape
    n_pairs = w // LANES
    assert n_pairs % pairs_per_step == 0 and t % RWKV_CHUNK == 0
    gw = pairs_per_step * LANES
    io_spec = pl.BlockSpec((1, RWKV_CHUNK, gw), lambda i, g, c: (i, c, g))
    return pl.pallas_call(
        functools.partial(_rwkv_chunk_kernel, pairs=pairs_per_step),
        grid=(bsz, n_pairs // pairs_per_step, t // RWKV_CHUNK),
        in_specs=[io_spec] * 6 + [pl.BlockSpec((8, gw), lambda i, g, c: (0, g))],
        out_specs=[io_spec, pl.BlockSpec((1, pairs_per_step, LANES, LANES), lambda i, g, c: (i, g, 0, 0))],
        out_shape=[jax.ShapeDtypeStruct((bsz, t, w), F32),
                   jax.ShapeDtypeStruct((bsz, n_pairs, LANES, LANES), F32)],
        scratch_shapes=[pltpu.VMEM((pairs_per_step, LANES, LANES), F32)],
        compiler_params=_cparams(("parallel", "parallel", "arbitrary")),
        name="rwkv_chunk_scan",
    )(r, lw, k, v, a, b, pvec)


def _rwkv_step_kernel(s_ref, r_ref, lw_ref, k_ref, a_ref, b_ref, v_ref, rk_ref, gw_ref, gb_ref, y_ref, so_ref):
    s = s_ref[0]
    r = r_ref[0]
    k = k_ref[0]
    v = v_ref[0]
    sa = jnp.sum(s * a_ref[0], axis=-1, keepdims=True)
    s_new = s * jnp.exp(lw_ref[0]) + sa * b_ref[0] + v * k
    so_ref[0] = s_new
    y = jnp.sum(s_new * r, axis=-1, keepdims=True)
    mean = jnp.mean(y, axis=1, keepdims=True)
    d = y - mean
    var = jnp.mean(d * d, axis=1, keepdims=True)
    bonus = jnp.sum(r * k * rk_ref[...], axis=-1, keepdims=True) * v
    y_ref[0] = d * lax.rsqrt(var + GN_EPS) * gw_ref[...] + gb_ref[...] + bonus


def rwkv_step(state, r, lw, k, v, a, b, r_k, gn_w, gn_b):
    bsz, h, n, _ = state.shape
    row = lambda x: x.reshape(bsz, h, 1, n)
    col = lambda x: x.reshape(bsz, h, n, 1)
    row_spec = pl.BlockSpec((1, h, 1, n), lambda i: (i, 0, 0, 0))
    col_spec = pl.BlockSpec((1, h, n, 1), lambda i: (i, 0, 0, 0))
    st_spec = pl.BlockSpec((1, h, n, n), lambda i: (i, 0, 0, 0))
    prow = pl.BlockSpec((h, 1, n), lambda i: (0, 0, 0))
    pcol = pl.BlockSpec((h, n, 1), lambda i: (0, 0, 0))
    y, s_new = pl.pallas_call(
        _rwkv_step_kernel,
        grid=(bsz,),
        in_specs=[st_spec, row_spec, row_spec, row_spec, row_spec, row_spec, col_spec, prow, pcol, pcol],
        out_specs=[col_spec, st_spec],
        out_shape=[jax.ShapeDtypeStruct((bsz, h, n, 1), F32), jax.ShapeDtypeStruct((bsz, h, n, n), F32)],
        compiler_params=_cparams(("parallel",)),
        name="rwkv_step",
    )(state, row(r), row(lw), row(k), row(a), row(b), col(v),
      r_k.reshape(h, 1, n), gn_w.reshape(h, n, 1), gn_b.reshape(h, n, 1))
    return y.reshape(bsz, 1, h * n), s_new


def _moba_prompt_kernel(slope_ref, q_ref, k_ref, v_ref, o_ref, ko_ref, vo_ref, kb_ref, vb_ref, kmean_ref, rel_ref):
    h = pl.program_id(1)
    qi = pl.program_id(2)
    blk = q_ref.shape[1]
    t = k_ref.shape[1]
    n_blk = t // blk
    d = q_ref.shape[2]
    slope = slope_ref[h]
    scale = d ** -0.5

    @pl.when(qi == 0)
    def _():
        kf = k_ref[0]
        vf = v_ref[0]
        ko_ref[0, 0] = kf
        vo_ref[0, 0] = vf
        kb_ref[...] = kf.astype(BF16)
        vb_ref[...] = vf.astype(BF16)
        kmean_ref[...] = jnp.sum(kf.reshape(n_blk, blk, d), axis=1) * (1.0 / blk)
        rel = (lax.broadcasted_iota(jnp.int32, (blk, blk), 0)
               - lax.broadcasted_iota(jnp.int32, (blk, blk), 1)).astype(F32)
        rel_ref[...] = -slope * rel

    q = q_ref[0]
    qb = q.astype(BF16)
    gate = _dot_nt_x3(q, kmean_ref[...])
    nidx = lax.broadcasted_iota(jnp.int32, (blk, n_blk), 1).astype(F32)
    neg_inf = jnp.full((blk, n_blk), -jnp.inf, F32)
    vals = jnp.where(nidx < qi.astype(F32), gate, neg_inf)
    sel_bias = jnp.full((blk, n_blk), NEG_BIG, F32)
    for _ in range(MOBA_TOPK):
        mx = jnp.max(vals, axis=1, keepdims=True)
        is_max = jnp.logical_and(vals == mx, mx > -jnp.inf)
        first = jnp.min(jnp.where(is_max, nidx, float(n_blk)), axis=1, keepdims=True)
        pick = nidx == first
        sel_bias = jnp.where(pick, 0.0, sel_bias)
        vals = jnp.where(pick, neg_inf, vals)
    sel_bias = sel_bias.astype(BF16)

    own = pl.multiple_of(qi * blk, blk)
    s = _bdot_nt(qb, kb_ref[pl.ds(own, blk), :]) * scale + rel_ref[...]
    causal = lax.broadcasted_iota(jnp.int32, (blk, blk), 1) <= lax.broadcasted_iota(jnp.int32, (blk, blk), 0)
    s = jnp.where(causal, s, NEG_BIG)
    m0 = jnp.max(s, axis=1, keepdims=True)
    p = jnp.exp(s - m0)
    l0 = jnp.sum(p, axis=1, keepdims=True)
    acc0 = jnp.dot(p.astype(BF16), vb_ref[pl.ds(own, blk), :], preferred_element_type=F32)
    blk_rows = lax.broadcasted_iota(jnp.int32, (n_blk, blk), 0)

    def body(n, carry):
        m, l, acc = carry
        start = pl.multiple_of(n * blk, blk)
        onehot = (blk_rows == n).astype(BF16)
        dist = ((qi - n) * blk).astype(F32)
        s = (_bdot_nt(qb, kb_ref[pl.ds(start, blk), :]) * scale + rel_ref[...] - slope * dist
             + jnp.dot(sel_bias, onehot, preferred_element_type=F32))
        m_new = jnp.maximum(m, jnp.max(s, axis=1, keepdims=True))
        alpha = jnp.exp(m - m_new)
        p = jnp.exp(s - m_new)
        l = alpha * l + jnp.sum(p, axis=1, keepdims=True)
        acc = alpha * acc + jnp.dot(p.astype(BF16), vb_ref[pl.ds(start, blk), :], preferred_element_type=F32)
        return m_new, l, acc

    _, l, acc = lax.fori_loop(0, qi, body, (m0, l0, acc0))
    o_ref[0] = acc / l


def moba_prompt(z3, slopes, *, heads):
    bsz, t, _ = z3.shape
    d = B_HEAD_DIM
    blk = MOBA_BLOCK
    assert t % blk == 0
    return pl.pallas_call(
        _moba_prompt_kernel,
        grid_spec=pltpu.PrefetchScalarGridSpec(
            num_scalar_prefetch=0,
            grid=(bsz, heads, t // blk),
            in_specs=[pl.BlockSpec(memory_space=pltpu.SMEM),
                      pl.BlockSpec((1, blk, d), lambda b, h, i: (b, i, h)),
                      pl.BlockSpec((1, t, d), lambda b, h, i: (b, 0, heads + h)),
                      pl.BlockSpec((1, t, d), lambda b, h, i: (b, 0, 2 * heads + h))],
            out_specs=[pl.BlockSpec((1, blk, d), lambda b, h, i: (b, i, h)),
                       pl.BlockSpec((1, 1, t, d), lambda b, h, i: (b, h, 0, 0)),
                       pl.BlockSpec((1, 1, t, d), lambda b, h, i: (b, h, 0, 0))],
            scratch_shapes=[pltpu.VMEM((t, d), BF16), pltpu.VMEM((t, d), BF16),
                            pltpu.VMEM((t // blk, d), F32), pltpu.VMEM((blk, blk), F32)],
        ),
        out_shape=[jax.ShapeDtypeStruct((bsz, t, heads * d), F32),
                   jax.ShapeDtypeStruct((bsz, heads, t, d), F32),
                   jax.ShapeDtypeStruct((bsz, heads, t, d), F32)],
        compiler_params=_cparams(("parallel", "parallel", "arbitrary")),
        name="moba_prompt",
    )(slopes, z3, z3, z3)


def _paged_block_sum_kernel(pt_ref, *refs, pages_per_block):
    o_ref = refs[-1]
    for j in range(o_ref.shape[1]):
        acc = jnp.sum(refs[j * pages_per_block][...], axis=1)
        for s in range(1, pages_per_block):
            acc = acc + jnp.sum(refs[j * pages_per_block + s][...], axis=1)
        o_ref[0, j] = acc


def paged_block_sums(cache_k, page_table, *, pages_per_block, blocks_per_step):
    _, h, page, d = cache_k.shape
    bsz, n_pages = page_table.shape
    n_blk = n_pages // pages_per_block
    assert n_blk % blocks_per_step == 0
    pages_per_step = pages_per_block * blocks_per_step

    def page_spec(s):
        return pl.BlockSpec((None, h, page, d), lambda b, n, pt: (pt[b, n * pages_per_step + s], 0, 0, 0))

    return pl.pallas_call(
        functools.partial(_paged_block_sum_kernel, pages_per_block=pages_per_block),
        grid_spec=pltpu.PrefetchScalarGridSpec(
            num_scalar_prefetch=1,
            grid=(bsz, n_blk // blocks_per_step),
            in_specs=[page_spec(s) for s in range(pages_per_step)],
            out_specs=pl.BlockSpec((1, blocks_per_step, h, d), lambda b, n, pt: (b, n, 0, 0)),
        ),
        out_shape=jax.ShapeDtypeStruct((bsz, n_blk, h, d), F32),
        compiler_params=_cparams(("parallel", "arbitrary")),
        name="paged_block_sums",
    )(page_table, *([cache_k] * pages_per_step))


def _moba_select_kernel(ks_ref, q_ref, o_ref, *, block_len):
    n_blk, h = ks_ref.shape[1], ks_ref.shape[2]
    kmean = ks_ref[0] * (1.0 / block_len)
    gate = jnp.sum(kmean * q_ref[0], axis=-1, keepdims=True)
    nidx = lax.broadcasted_iota(jnp.int32, (n_blk, h, 1), 0).astype(F32)
    vals = gate
    for t in range(MOBA_TOPK):
        mx = jnp.max(vals, axis=0, keepdims=True)
        first = jnp.min(jnp.where(vals == mx, nidx, float(n_blk)), axis=0, keepdims=True)
        o_ref[0, t] = first[0].astype(jnp.int32)
        vals = jnp.where(nidx == first, -jnp.inf, vals)


def moba_select(ksum, q, *, block_len):
    bsz, n_blk, h, d = ksum.shape
    return pl.pallas_call(
        functools.partial(_moba_select_kernel, block_len=block_len),
        grid=(bsz,),
        in_specs=[pl.BlockSpec((1, n_blk, h, d), lambda b: (b, 0, 0, 0)),
                  pl.BlockSpec((1, h, d), lambda b: (b, 0, 0))],
        out_specs=pl.BlockSpec((1, MOBA_TOPK, h, 1), lambda b: (b, 0, 0, 0)),
        out_shape=jax.ShapeDtypeStruct((bsz, MOBA_TOPK, h, 1), jnp.int32),
        compiler_params=_cparams(("parallel",)),
        name="moba_select",
    )(ksum, q)


def _moba_decode_kernel(pt_ref, sel_ref, slope_ref, q_ref, kn_ref, vn_ref, ck_ref, cv_ref, o_ref,
                        kbuf, vbuf, sems, *, past_len, page, heads):
    b = pl.program_id(0)
    d = B_HEAD_DIM
    scale = d ** -0.5
    pages_per_block = MOBA_BLOCK // page
    n_slots = MOBA_TOPK * pages_per_block

    def page_copies(h, s):
        blk = sel_ref[b, s // pages_per_block, h]
        pg = pt_ref[b, blk * pages_per_block + s % pages_per_block]
        return (pltpu.make_async_copy(ck_ref.at[pg, h], kbuf.at[h, s], sems.at[0, h]),
                pltpu.make_async_copy(cv_ref.at[pg, h], vbuf.at[h, s], sems.at[1, h]))

    for h in range(heads):
        for s in range(n_slots):
            for cp in page_copies(h, s):
                cp.start()

    lane = lax.broadcasted_iota(jnp.int32, (1, page), 1)
    for h in range(heads):
        for s in range(n_slots):
            for cp in page_copies(h, s):
                cp.wait()
        sl = slice(h * d, (h + 1) * d)
        slope = slope_ref[h]
        q8 = jnp.broadcast_to(q_ref[0, :, sl], (8, d))
        logits = []
        for s in range(n_slots):
            blk = sel_ref[b, s // pages_per_block, h]
            kpos = blk * MOBA_BLOCK + (s % pages_per_block) * page + lane
            sc = _mm_nt(q8, kbuf[h, s], True)[0:1, :] * scale
            logits.append(sc - slope * (past_len - kpos).astype(F32))
        own = jnp.sum(q8[0:1, :] * kn_ref[0, :, sl], axis=1, keepdims=True) * scale
        m = own
        for sc in logits:
            m = jnp.maximum(m, jnp.max(sc, axis=1, keepdims=True))
        p_own = jnp.exp(own - m)
        l = p_own
        acc = p_own * vn_ref[0, :, sl]
        for s in range(n_slots):
            p = jnp.exp(logits[s] - m)
            l = l + jnp.sum(p, axis=1, keepdims=True)
            acc = acc + _mm(jnp.broadcast_to(p, (8, page)), vbuf[h, s], True)[0:1, :]
        o_ref[0, :, sl] = acc / l


def moba_decode(z3, cache_k, cache_v, page_table, sel, slopes, *, past_len):
    bsz = z3.shape[0]
    _, heads, page, d = cache_k.shape
    w = heads * d
    n_slots = MOBA_TOPK * (MOBA_BLOCK // page)
    row_spec = lambda col: pl.BlockSpec((1, 1, w), lambda b, pt, sel_: (b, 0, col))
    return pl.pallas_call(
        functools.partial(_moba_decode_kernel, past_len=past_len, page=page, heads=heads),
        grid_spec=pltpu.PrefetchScalarGridSpec(
            num_scalar_prefetch=2,
            grid=(bsz,),
            in_specs=[pl.BlockSpec(memory_space=pltpu.SMEM), row_spec(0), row_spec(1), row_spec(2),
                      pl.BlockSpec(memory_space=pl.ANY), pl.BlockSpec(memory_space=pl.ANY)],
            out_specs=pl.BlockSpec((1, 1, w), lambda b, pt, sel_: (b, 0, 0)),
            scratch_shapes=[pltpu.VMEM((heads, n_slots, page, d), F32),
                            pltpu.VMEM((heads, n_slots, page, d), F32),
                            pltpu.SemaphoreType.DMA((2, heads))],
        ),
        out_shape=jax.ShapeDtypeStruct((bsz, 1, w), F32),
        compiler_params=_cparams(("arbitrary",)),
        name="moba_decode",
    )(page_table, sel, slopes, z3, z3, z3, cache_k, cache_v)


def _mem_attn_kernel(q_ref, k_ref, v_ref, o_ref, *, precise):
    tq = q_ref.shape[1]
    scale = C_HEAD_DIM ** -0.5
    rows = max(tq, 8)
    for h in range(C_HEADS):
        sl = slice(h * C_HEAD_DIM, (h + 1) * C_HEAD_DIM)
        q = jnp.broadcast_to(q_ref[0, :, sl], (rows, C_HEAD_DIM)) if tq < 8 else q_ref[0, :, sl]
        s = _mm_nt(q, k_ref[0, :, sl], precise) * scale
        m = jnp.max(s, axis=1, keepdims=True)
        p = jnp.exp(s - m)
        l = jnp.sum(p, axis=1, keepdims=True)
        o = _mm(p, v_ref[0, :, sl], precise) / l
        o_ref[0, :, sl] = o[0:tq, :]


def mem_attention(z3, mem_k3, mem_v3, *, q_col, k_col, v_col, tq, precise):
    bsz, t, _ = z3.shape
    mlen = mem_k3.shape[1]
    return pl.pallas_call(
        functools.partial(_mem_attn_kernel, precise=precise),
        grid=(bsz, t // tq),
        in_specs=[pl.BlockSpec((1, tq, C_WIDTH), lambda b, i: (b, i, q_col)),
                  pl.BlockSpec((1, mlen, C_WIDTH), lambda b, i: (b, 0, k_col)),
                  pl.BlockSpec((1, mlen, C_WIDTH), lambda b, i: (b, 0, v_col))],
        out_specs=pl.BlockSpec((1, tq, C_WIDTH), lambda b, i: (b, i, 0)),
        out_shape=jax.ShapeDtypeStruct((bsz, t, C_WIDTH), F32),
        compiler_params=_cparams(("parallel", "parallel")),
        name="mem_attention",
    )(z3, mem_k3, mem_v3)


def _out_proj_kernel(ymix_ref, ymem_ref, g0_ref, g1_ref, w_ref, g_ref, x_ref, o_ref, *, precise):
    half = g0_ref.shape[1]
    wmix = ymix_ref.shape[1]
    silu = lambda t: t * _sigmoid(t)
    g1 = g1_ref[...]
    u0 = ymix_ref[:, 0:half] * silu(g0_ref[...])
    u1 = ymix_ref[:, half:wmix] * silu(g1[:, 0:wmix - half])
    u2 = ymem_ref[...] * silu(g1[:, wmix - half:])
    y = _mm(u0, w_ref[0:half, :], precise)
    y = y + _mm(u1, w_ref[half:wmix, :], precise)
    y = y + _mm(u2, w_ref[wmix:, :], precise)
    o_ref[...] = x_ref[...] + _rms_scale(y, g_ref[...])


def out_proj(ymix, ymem, z, w_out, g, x, *, gate_col, tm):
    m, dm = x.shape
    wmix = ymix.shape[1]
    half = 1024
    return pl.pallas_call(
        functools.partial(_out_proj_kernel, precise=w_out.dtype == F32),
        grid=(m // tm,),
        in_specs=[pl.BlockSpec((tm, wmix), lambda i: (i, 0)),
                  pl.BlockSpec((tm, C_WIDTH), lambda i: (i, 0)),
                  pl.BlockSpec((tm, half), lambda i: (i, gate_col)),
                  pl.BlockSpec((tm, half), lambda i: (i, gate_col + 1)),
                  pl.BlockSpec(w_out.shape, lambda i: (0, 0)),
                  pl.BlockSpec((1, dm), lambda i: (0, 0)),
                  pl.BlockSpec((tm, dm), lambda i: (i, 0))],
        out_specs=pl.BlockSpec((tm, dm), lambda i: (i, 0)),
        out_shape=jax.ShapeDtypeStruct((m, dm), F32),
        compiler_params=_cparams(("parallel",)),
        name="out_proj",
    )(ymix, ymem, z, z, w_out, g, x)


def _prep_rwkv_weights(w_in, mu, decay_b, iclr_b, vres_b, a_width):
    rkv = 3 * a_width
    n_lora = mu.shape[0] - rkv
    pad = LORA_PAD - n_lora
    dm = w_in.shape[0]
    w = jnp.concatenate([w_in[:, :rkv], w_in[:, rkv + n_lora:], w_in[:, rkv:rkv + n_lora],
                         jnp.zeros((dm, pad), w_in.dtype)], axis=1)
    mu_r = mu[:rkv].reshape(1, rkv)
    mu_l = jnp.pad(mu[rkv:], (0, pad)).reshape(1, LORA_PAD)

    def pad_rows(m, start):
        return jnp.pad(m, ((start, LORA_PAD - start - m.shape[0]), (0, 0)))

    wd = pad_rows(decay_b, 0)
    wi = pad_rows(iclr_b, DECAY_LORA)
    wv = None if vres_b is None else pad_rows(vres_b, DECAY_LORA + ICLR_LORA)
    return w, mu_r, mu_l, wd, wi, wv


def kernel(x_prompt, x_sample, state_wkv, state_shift, cache_k_l1, cache_v_l1, cache_k_l3, cache_v_l3, cache_mem_k, cache_mem_v, page_table, mem_prompt, norm_pre, norm_post, w_in_l0, w_in_l1, w_in_l2, w_in_l3, w_mem_kv, w_out, mu_l0, mu_l2, rwkv_decay_b, rwkv_decay_0, rwkv_iclr_b, rwkv_iclr_0, rwkv_k_k, rwkv_k_a, rwkv_r_k, rwkv_gn_w, rwkv_gn_b, vres_b_l2, vres_0_l2):
    depth = norm_pre.shape[0]
    bp, seq, dm = x_prompt.shape
    bs, dec_seq, _ = x_sample.shape
    assert dec_seq == 1
    a_heads = state_wkv.shape[2]
    a_width = a_heads * A_HEAD_DIM
    n_pool, page, b_heads, _ = cache_k_l1.shape
    b_width = b_heads * B_HEAD_DIM
    mem_len = mem_prompt.shape[1]
    past_len = page_table.shape[1] * page
    assert past_len % MOBA_BLOCK == 0 and past_len // MOBA_BLOCK >= MOBA_TOPK and MOBA_BLOCK % page == 0
    assert a_width == b_width
    mix_w = a_width
    qmem_col = 3 * mix_w // C_WIDTH
    gate_col = (3 * mix_w + C_WIDTH) // 1024
    lora_col = (3 * mix_w + C_WIDTH + dm) // LORA_PAD

    zeros8 = jnp.zeros((a_width,), F32)
    rw = []
    for j, (w_in, mu, vb, v0) in enumerate(((w_in_l0, mu_l0, None, None), (w_in_l2, mu_l2, vres_b_l2, vres_0_l2))):
        w, mu_r, mu_l, wd, wi, wv = _prep_rwkv_weights(w_in, mu, rwkv_decay_b[j], rwkv_iclr_b[j], vb, a_width)
        pvec = jnp.stack([rwkv_decay_0[j], rwkv_iclr_0[j], zeros8 if v0 is None else v0, rwkv_k_k[j], rwkv_k_a[j],
                          zeros8, zeros8, zeros8])
        pvec2 = jnp.stack([rwkv_r_k[j].reshape(-1), rwkv_gn_w[j], rwkv_gn_b[j]] + [zeros8] * 5)
        lora = {True: (wd.astype(BF16), wi.astype(BF16), None if wv is None else wv.astype(BF16)),
                False: (wd, wi, wv)}
        rw.append(dict(w={True: w.astype(BF16), False: w}, lora=lora, mu_r=mu_r, mu_l=mu_l, pvec=pvec, pvec2=pvec2))
    w_moba = [{True: w.astype(BF16), False: w} for w in (w_in_l1, w_in_l3)]
    w_out_p = {True: w_out.astype(BF16), False: w_out}
    w_mem_all = jnp.transpose(w_mem_kv, (1, 0, 2)).reshape(dm, depth * 2 * C_WIDTH).astype(BF16)
    slopes = jnp.asarray(alibi_slopes(b_heads))
    ones_g = jnp.ones((1, dm), F32)
    caches = [tuple(jnp.transpose(c, (0, 2, 1, 3)) for c in pair)
              for pair in ((cache_k_l1, cache_v_l1), (cache_k_l3, cache_v_l3))]

    mem_rows = mem_prompt.reshape(bp * mem_len, dm)
    memkv = norm_matmul(mem_rows, ones_g, w_mem_all, normalize=False, tm=bp * mem_len, tn=512)
    memkv3 = memkv.reshape(bp, mem_len, depth * 2 * C_WIDTH)
    memkv5 = memkv.reshape(bp, mem_len, depth, 2, C_HEADS, C_HEAD_DIM)
    p_mem_k = jnp.transpose(memkv5[:, :, :, 0], (2, 0, 1, 3, 4))
    p_mem_v = jnp.transpose(memkv5[:, :, :, 1], (2, 0, 1, 3, 4))
    cache_mem_k3 = cache_mem_k.reshape(depth, bs, mem_len, C_WIDTH)
    cache_mem_v3 = cache_mem_v.reshape(depth, bs, mem_len, C_WIDTH)

    def trunk(x3, is_prompt):
        bsz, t, _ = x3.shape
        m = bsz * t
        x = x3.reshape(m, dm)
        tm_a = 1024 if m % 1024 == 0 else m
        tm_f = 256 if m % 256 == 0 else m
        v_first = None
        wkv_out, shift_out, kv_out = [], [], []
        for i in range(depth):
            j = i // 2
            g_pre = norm_pre[i].reshape(1, dm)
            if i % 2 == 0:
                p = rw[j]
                z = norm_matmul(x, g_pre, p["w"][is_prompt], normalize=True, tm=tm_a, tn=512)
                z3 = z.reshape(bsz, t, -1)
                if is_prompt:
                    zprev3 = jnp.zeros((bsz, 1, z.shape[1]), F32)
                else:
                    zprev3 = norm_matmul(state_shift[j], ones_g, p["w"][is_prompt], normalize=False, tm=bsz, tn=512
                                         ).reshape(bsz, 1, -1)
                wd, wi, wv = p["lora"][is_prompt]
                r, lw, k, v, a, b = rwkv_prep(z3, zprev3, p["mu_r"], p["mu_l"], wd, wi, wv, p["pvec"],
                                              v_first, tt=min(t, 128), width=a_width, lora_col=lora_col)
                if v_first is None:
                    v_first = v
                if is_prompt:
                    y_mix, h_bd = rwkv_chunk_scan(r, lw, k, v, a, b, p["pvec2"], pairs_per_step=4)
                    hb = h_bd.reshape(bsz, a_heads // 2, 2, A_HEAD_DIM, 2, A_HEAD_DIM)
                    s_new = jnp.stack([hb[:, :, 0, :, 0, :], hb[:, :, 1, :, 1, :]], axis=2)
                    s_new = jnp.swapaxes(s_new.reshape(bsz, a_heads, A_HEAD_DIM, A_HEAD_DIM), -1, -2)
                else:
                    y_mix, s_new = rwkv_step(state_wkv[j], r, lw, k, v, a, b, rwkv_r_k[j], rwkv_gn_w[j], rwkv_gn_b[j])
                wkv_out.append(s_new)
                shift_out.append(rmsnorm_rows(x3[:, -1], g_pre))
            else:
                z = norm_matmul(x, g_pre, w_moba[j][is_prompt], normalize=True, tm=tm_a, tn=512)
                z3 = z.reshape(bsz, t, -1)
                if is_prompt:
                    y_mix, k_hm, v_hm = moba_prompt(z3, slopes, heads=b_heads)
                    kv_out.append((jnp.transpose(k_hm, (0, 2, 1, 3)), jnp.transpose(v_hm, (0, 2, 1, 3))))
                else:
                    ck, cv = caches[j]
                    ksum = paged_block_sums(ck, page_table, pages_per_block=MOBA_BLOCK // page, blocks_per_step=4)
                    sel = moba_select(ksum, z3[:, 0, :b_width].reshape(bsz, b_heads, B_HEAD_DIM),
                                      block_len=MOBA_BLOCK)
                    y_mix = moba_decode(z3, ck, cv, page_table, sel[..., 0], slopes, past_len=past_len)
                    kv_out.append((z3[:, :, b_width:2 * b_width].reshape(bsz, t, b_heads, B_HEAD_DIM),
                                   z3[:, :, 2 * b_width:3 * b_width].reshape(bsz, t, b_heads, B_HEAD_DIM)))
            if is_prompt:
                y_mem = mem_attention(z3, memkv3, memkv3, q_col=qmem_col, k_col=2 * i, v_col=2 * i + 1,
                                      tq=min(t, 512), precise=False)
            else:
                y_mem = mem_attention(z3, cache_mem_k3[i], cache_mem_v3[i], q_col=qmem_col, k_col=0, v_col=0, tq=t,
                                      precise=True)
            x = out_proj(y_mix.reshape(m, mix_w), y_mem.reshape(m, C_WIDTH), z, w_out_p[is_prompt][i],
                         norm_post[i].reshape(1, dm), x, gate_col=gate_col, tm=tm_f)
            x3 = x.reshape(bsz, t, dm)
        return x3, jnp.stack(wkv_out), jnp.stack(shift_out), kv_out

    y_prompt, p_wkv, p_shift, p_kv = trunk(x_prompt, True)
    y_sample, s_wkv, s_shift, s_kv = trunk(x_sample, False)
    (p_k_l1, p_v_l1), (p_k_l3, p_v_l3) = p_kv
    (s_k_l1, s_v_l1), (s_k_l3, s_v_l3) = s_kv
    return (y_prompt, y_sample, p_wkv, p_shift, p_k_l1, p_v_l1, p_k_l3, p_v_l3, p_mem_k, p_mem_v,
            s_wkv, s_shift, s_k_l1, s_v_l1, s_k_l3, s_v_l3)
```

```python
import functools
import math

import numpy as np
import jax
import jax.numpy as jnp
from jax import lax
from jax.experimental import pallas as pl
from jax.experimental.pallas import tpu as pltpu

F32 = jnp.float32
BF16 = jnp.bfloat16

RMS_EPS = 1e-6
GN_EPS = 64e-5
A_HEAD_DIM = 64
B_HEAD_DIM = 128
C_HEADS = 4
C_HEAD_DIM = 128
C_WIDTH = C_HEADS * C_HEAD_DIM
MOBA_BLOCK = 256
MOBA_TOPK = 3
DECAY_LORA = 96
ICLR_LORA = 96
VRES_LORA = 64

LANES = 128
RWKV_CHUNK = 64
LORA_PAD = 512
NEG_BIG = -1e30
MOBA_ROWS = 128
MOBA_KGROUP = 2
VMEM_LIMIT = 48 * 1024 * 1024


def _cparams(sem):
    return pltpu.CompilerParams(dimension_semantics=sem, vmem_limit_bytes=VMEM_LIMIT)


def alibi_slopes(n):
    def pow2_slopes(m):
        start = 2.0 ** (-(2.0 ** -(math.log2(m) - 3)))
        return [start ** (i + 1) for i in range(m)]
    if math.log2(n).is_integer():
        s = pow2_slopes(n)
    else:
        c = 2 ** math.floor(math.log2(n))
        s = pow2_slopes(c) + pow2_slopes(2 * c)[0::2][: n - c]
    return np.asarray(s, dtype=np.float32)


def _bdot(a, b):
    return jnp.dot(a.astype(BF16), b.astype(BF16), preferred_element_type=F32)


def _bdot_nt(a, b):
    return lax.dot_general(a.astype(BF16), b.astype(BF16), (((1,), (1,)), ((), ())),
                           preferred_element_type=F32)


def _mm(a, b, precise):
    if precise:
        return jnp.dot(a, b, preferred_element_type=F32, precision=lax.Precision.HIGHEST)
    return _bdot(a, b)


def _mm_nt(a, b, precise):
    if precise:
        return lax.dot_general(a, b, (((1,), (1,)), ((), ())), preferred_element_type=F32,
                               precision=lax.Precision.HIGHEST)
    return _bdot_nt(a, b)


def _split3(x):
    hi = x.astype(BF16)
    r1 = x - hi.astype(F32)
    mid = r1.astype(BF16)
    lo = (r1 - mid.astype(F32)).astype(BF16)
    return hi, mid, lo


def _dot_rhs01(x, q01):
    hi, mid, lo = _split3(x)
    acc = jnp.dot(lo, q01, preferred_element_type=F32)
    acc = acc + jnp.dot(mid, q01, preferred_element_type=F32)
    return acc + jnp.dot(hi, q01, preferred_element_type=F32)


def _dot_lhs01(t01, x):
    hi, mid, lo = _split3(x)
    acc = jnp.dot(t01, lo, preferred_element_type=F32)
    acc = acc + jnp.dot(t01, mid, preferred_element_type=F32)
    return acc + jnp.dot(t01, hi, preferred_element_type=F32)


def _dot_nt_x3(a, b):
    ah = a.astype(BF16)
    al = (a - ah.astype(F32)).astype(BF16)
    bh = b.astype(BF16)
    bl = (b - bh.astype(F32)).astype(BF16)
    dn = (((1,), (1,)), ((), ()))
    acc = lax.dot_general(al, bh, dn, preferred_element_type=F32)
    acc = acc + lax.dot_general(ah, bl, dn, preferred_element_type=F32)
    return acc + lax.dot_general(ah, bh, dn, preferred_element_type=F32)


def _head_sum_matrix(width, head):
    r = lax.broadcasted_iota(jnp.int32, (width, width), 0) // head
    c = lax.broadcasted_iota(jnp.int32, (width, width), 1) // head
    return (r == c).astype(BF16)


def _sigmoid(x):
    return 1.0 / (1.0 + jnp.exp(-x))


def _softplus(x):
    return jnp.maximum(x, 0.0) + jnp.log(1.0 + jnp.exp(-jnp.abs(x)))


def _rms_scale(x, g):
    ms = jnp.mean(x * x, axis=-1, keepdims=True)
    return x * lax.rsqrt(ms + RMS_EPS) * g


def _norm_matmul_kernel(x_ref, g_ref, w_ref, o_ref, h_ref, *, normalize, row_block, precise):
    @pl.when(pl.program_id(1) == 0)
    def _():
        tm = x_ref.shape[0]
        for s in range(0, tm, row_block):
            x = x_ref[s:s + row_block, :]
            if normalize:
                x = _rms_scale(x, g_ref[...])
            h_ref[s:s + row_block, :] = x.astype(h_ref.dtype)
    o_ref[...] = _mm(h_ref[...], w_ref[...], precise)


def norm_matmul(x, g, w, *, normalize, tm, tn):
    m, k = x.shape
    n = w.shape[1]
    assert m % tm == 0 and n % tn == 0, (m, tm, n, tn)
    row_block = min(tm, 256)
    precise = w.dtype == F32
    return pl.pallas_call(
        functools.partial(_norm_matmul_kernel, normalize=normalize, row_block=row_block, precise=precise),
        grid=(m // tm, n // tn),
        in_specs=[pl.BlockSpec((tm, k), lambda i, j: (i, 0)),
                  pl.BlockSpec((1, k), lambda i, j: (0, 0)),
                  pl.BlockSpec((k, tn), lambda i, j: (0, j))],
        out_specs=pl.BlockSpec((tm, tn), lambda i, j: (i, j)),
        out_shape=jax.ShapeDtypeStruct((m, n), F32),
        scratch_shapes=[pltpu.VMEM((tm, k), w.dtype)],
        compiler_params=_cparams(("parallel", "arbitrary")),
        name="norm_matmul" if normalize else "plain_matmul",
    )(x, g, w)


def _rmsnorm_rows_kernel(x_ref, g_ref, o_ref):
    o_ref[...] = _rms_scale(x_ref[...], g_ref[...])


def rmsnorm_rows(x, g):
    return pl.pallas_call(
        _rmsnorm_rows_kernel,
        out_shape=jax.ShapeDtypeStruct(x.shape, F32),
        name="rmsnorm_rows",
    )(x, g)


def _rwkv_prep_kernel(*refs, has_vres, width, precise):
    if has_vres:
        (zr_ref, zl_ref, pfr_ref, pfl_ref, mur_ref, mul_ref, wd_ref, wi_ref, wv_ref, pv_ref, vf_ref,
         r_o, lw_o, k_o, v_o, a_o, b_o, carry_r, carry_l) = refs
    else:
        (zr_ref, zl_ref, pfr_ref, pfl_ref, mur_ref, mul_ref, wd_ref, wi_ref, pv_ref,
         r_o, lw_o, k_o, v_o, a_o, b_o, carry_r, carry_l) = refs
        wv_ref = vf_ref = None
    tt = zr_ref.shape[1]

    @pl.when(pl.program_id(1) == 0)
    def _():
        carry_r[...] = pfr_ref[0]
        carry_l[...] = pfl_ref[0]

    if tt > 1:
        first_row = lax.broadcasted_iota(jnp.int32, (tt, 1), 0) == 0

    def shift_mix(z, carry, mu):
        if tt > 1:
            prev = jnp.where(first_row, carry, pltpu.roll(z, 1, 0))
        else:
            prev = carry
        return z + mu * (prev - z)

    zl = zl_ref[0]
    ls = shift_mix(zl, carry_l[...], mul_ref[...])
    carry_l[...] = zl[tt - 1:tt, :]

    decay_0 = pv_ref[0:1, :]
    iclr_0 = pv_ref[1:2, :]
    vres_0 = pv_ref[2:3, :]
    k_k = pv_ref[3:4, :]
    k_a = pv_ref[4:5, :]

    log_w = -_softplus(-(decay_0 + _mm(jnp.tanh(ls), wd_ref[...], precise))) - 0.5
    lw_o[0] = -jnp.exp(log_w)
    rate = _sigmoid(iclr_0 + _mm(ls, wi_ref[...], precise))

    def seg(i):
        z = zr_ref[0, :, i * width:(i + 1) * width]
        out = shift_mix(z, carry_r[:, i * width:(i + 1) * width], mur_ref[:, i * width:(i + 1) * width])
        carry_r[:, i * width:(i + 1) * width] = z[tt - 1:tt, :]
        return out

    r_o[0] = seg(0)
    k = seg(1)
    v = seg(2)
    if has_vres:
        v = v + (vf_ref[0] - v) * _sigmoid(vres_0 + _mm(ls, wv_ref[...], precise))
    v_o[0] = v

    q01 = _head_sum_matrix(LANES, A_HEAD_DIM)
    kk = k * k_k
    for p in range(width // LANES):
        sl = slice(p * LANES, (p + 1) * LANES)
        kp = kk[:, sl]
        norm = jnp.sqrt(_dot_rhs01(kp * kp, q01))
        kp = kp / jnp.maximum(norm, 1e-12)
        a_o[0, :, sl] = -kp
        b_o[0, :, sl] = kp * rate[:, sl]
    k_o[0] = k * (1.0 + (rate - 1.0) * k_a)


def rwkv_prep(z3, zprev3, mu_r, mu_l, wd, wi, wv, pvec, v_first, *, tt, width, lora_col):
    b, t, _ = z3.shape
    has_vres = wv is not None
    rkv = 3 * width
    in_specs = [
        pl.BlockSpec((1, tt, rkv), lambda i, j: (i, j, 0)),
        pl.BlockSpec((1, tt, LORA_PAD), lambda i, j: (i, j, lora_col)),
        pl.BlockSpec((1, 1, rkv), lambda i, j: (i, 0, 0)),
        pl.BlockSpec((1, 1, LORA_PAD), lambda i, j: (i, 0, lora_col)),
        pl.BlockSpec((1, rkv), lambda i, j: (0, 0)),
        pl.BlockSpec((1, LORA_PAD), lambda i, j: (0, 0)),
        pl.BlockSpec((LORA_PAD, width), lambda i, j: (0, 0)),
        pl.BlockSpec((LORA_PAD, width), lambda i, j: (0, 0)),
    ]
    args = [z3, z3, zprev3, zprev3, mu_r, mu_l, wd, wi]
    if has_vres:
        in_specs.append(pl.BlockSpec((LORA_PAD, width), lambda i, j: (0, 0)))
        args.append(wv)
    in_specs.append(pl.BlockSpec((8, width), lambda i, j: (0, 0)))
    args.append(pvec)
    if has_vres:
        in_specs.append(pl.BlockSpec((1, tt, width), lambda i, j: (i, j, 0)))
        args.append(v_first)
    out_spec = pl.BlockSpec((1, tt, width), lambda i, j: (i, j, 0))
    out_sds = jax.ShapeDtypeStruct((b, t, width), F32)
    return pl.pallas_call(
        functools.partial(_rwkv_prep_kernel, has_vres=has_vres, width=width, precise=wd.dtype == F32),
        grid=(b, t // tt),
        in_specs=in_specs,
        out_specs=[out_spec] * 6,
        out_shape=[out_sds] * 6,
        scratch_shapes=[pltpu.VMEM((1, rkv), F32), pltpu.VMEM((1, LORA_PAD), F32)],
        compiler_params=_cparams(("parallel", "arbitrary")),
        name="rwkv_prep",
    )(*args)


def _stack_heads(x, lane_is_head0):
    zero = jnp.zeros_like(x)
    return jnp.concatenate([jnp.where(lane_is_head0, x, zero), jnp.where(lane_is_head0, zero, x)], axis=0)


def _rwkv_group_chunk(rs, lws, ks, vs, as_, bs, hs, c):
    L = rs[0].shape[0]
    n2 = 2 * L
    each = lambda f, *xs: [f(*t) for t in zip(*xs)]
    st = functools.partial(_stack_heads, lane_is_head0=c["head0"])
    zero = jnp.zeros((n2, n2), F32)
    cum = [_dot_lhs01(c["tri"], lw) for lw in lws]
    tot = [x[L - 1:L, :] for x in cum]
    a_st = each(lambda a, cm, lw: st(a * jnp.exp(cm - lw)), as_, cum, lws)
    r_st = each(lambda r, cm: st(r * jnp.exp(cm)), rs, cum)
    v_st = [st(v) for v in vs]
    g_inv = [jnp.exp(-cm) for cm in cum]
    rhs = each(lambda b, k, g: jnp.concatenate([st(b * g), st(k * g)], axis=0), bs, ks, g_inv)
    gram = each(lambda a, r, x: _bdot_nt(jnp.concatenate([a, r], axis=0), x), a_st, r_st, rhs)
    a_ab = [jnp.where(c["strict"], g[0:n2, 0:n2], zero) for g in gram]
    a_ak = [jnp.where(c["strict"], g[0:n2, n2:2 * n2], zero) for g in gram]
    a_rb = [jnp.where(c["incl"], g[n2:2 * n2, 0:n2], zero) for g in gram]
    a_rk = [jnp.where(c["incl"], g[n2:2 * n2, n2:2 * n2], zero) for g in gram]
    pw = a_ab
    inv = [c["eye"] + x for x in a_ab]
    n = 1
    while 2 * n < L:
        pw = [_bdot(x, x) for x in pw]
        inv = each(lambda i, p: i + _bdot(i, p), inv, pw)
        n *= 2
    a_p = each(_bdot, inv, a_st)
    av = each(_bdot, a_ak, v_st)
    u0 = each(_bdot, inv, av)
    uv = each(lambda u, v: jnp.concatenate([u, v], axis=0), u0, v_st)
    g_end = each(lambda t, cm: jnp.exp(t - cm), tot, cum)
    bk_end_t = each(lambda b, k, g: jnp.concatenate([st(b * g), st(k * g)], axis=0).T, bs, ks, g_end)
    m_bd = each(lambda t, bt, ap: c["eye"] * jnp.exp(t) + _bdot(bt[:, 0:n2], ap), tot, bk_end_t, a_p)
    c_bd = each(_bdot, bk_end_t, uv)
    r_p = each(lambda r, arb, ap: r + _bdot(arb, ap), r_st, a_rb, a_p)
    y0 = each(lambda arb, ark, x: _bdot(jnp.concatenate([arb, ark], axis=1), x), a_rb, a_rk, uv)
    y_st = each(lambda rp, h, y: _bdot(rp, h) + y, r_p, hs, y0)
    h_new = each(lambda m, h, cc: _bdot(m, h) + cc, m_bd, hs, c_bd)
    return [y[0:L, :] + y[L:n2, :] for y in y_st], h_new


def _rwkv_chunk_kernel(r_ref, lw_ref, k_ref, v_ref, a_ref, b_ref, pv_ref, y_ref, hout_ref, h_ref, *, pairs):
    L = r_ref.shape[1]
    n2 = 2 * L

    @pl.when(pl.program_id(2) == 0)
    def _():
        h_ref[...] = jnp.zeros_like(h_ref)

    ri = lax.broadcasted_iota(jnp.int32, (n2, n2), 0)
    ci = lax.broadcasted_iota(jnp.int32, (n2, n2), 1)
    consts = {
        "tri": (lax.broadcasted_iota(jnp.int32, (L, L), 0) >= lax.broadcasted_iota(jnp.int32, (L, L), 1)).astype(BF16),
        "head0": lax.broadcasted_iota(jnp.int32, (1, LANES), 1) < A_HEAD_DIM,
        "strict": (ri % L) > (ci % L),
        "incl": (ri % L) >= (ci % L),
        "eye": (ri == ci).astype(F32),
    }
    sls = [slice(p * LANES, (p + 1) * LANES) for p in range(pairs)]
    load = lambda ref: [ref[0, :, sl] for sl in sls]
    rs, ks, vs = load(r_ref), load(k_ref), load(v_ref)
    ys, h_new = _rwkv_group_chunk(rs, load(lw_ref), ks, vs, load(a_ref), load(b_ref),
                                  [h_ref[p] for p in range(pairs)], consts)
    for p in range(pairs):
        h_ref[p] = h_new[p]
    q01 = _head_sum_matrix(LANES, A_HEAD_DIM)
    inv_n = 1.0 / A_HEAD_DIM
    mean = [_dot_rhs01(y, q01) * inv_n for y in ys]
    dev = [y - m for y, m in zip(ys, mean)]
    var = [_dot_rhs01(d * d, q01) * inv_n for d in dev]
    bonus = [_dot_rhs01(r * k * pv_ref[0:1, sl], q01) * v for r, k, v, sl in zip(rs, ks, vs, sls)]
    for p, sl in enumerate(sls):
        y_ref[0, :, sl] = dev[p] * lax.rsqrt(var[p] + GN_EPS) * pv_ref[1:2, sl] + pv_ref[2:3, sl] + bonus[p]

    @pl.when(pl.program_id(2) == pl.num_programs(2) - 1)
    def _():
        hout_ref[0] = h_ref[...]


def rwkv_chunk_scan(r, lw, k, v, a, b, pvec, *, pairs_per_step):
    bsz, t, w = r.shape
    n_pairs = w // LANES
    assert n_pairs % pairs_per_step == 0 and t % RWKV_CHUNK == 0
    gw = pairs_per_step * LANES
    io_spec = pl.BlockSpec((1, RWKV_CHUNK, gw), lambda i, g, c: (i, c, g))
    return pl.pallas_call(
        functools.partial(_rwkv_chunk_kernel, pairs=pairs_per_step),
        grid=(bsz, n_pairs // pairs_per_step, t // RWKV_CHUNK),
        in_specs=[io_spec] * 6 + [pl.BlockSpec((8, gw), lambda i, g, c: (0, g))],
        out_specs=[io_spec, pl.BlockSpec((1, pairs_per_step, LANES, LANES), lambda i, g, c: (i, g, 0, 0))],
        out_shape=[jax.ShapeDtypeStruct((bsz, t, w), F32),
                   jax.ShapeDtypeStruct((bsz, n_pairs, LANES, LANES), F32)],
        scratch_shapes=[pltpu.VMEM((pairs_per_step, LANES, LANES), F32)],
        compiler_params=_cparams(("parallel", "parallel", "arbitrary")),
        name="rwkv_chunk_scan",
    )(r, lw, k, v, a, b, pvec)


def _rwkv_step_kernel(s_ref, r_ref, lw_ref, k_ref, a_ref, b_ref, v_ref, rk_ref, gw_ref, gb_ref, y_ref, so_ref):
    s = s_ref[0]
    r = r_ref[0]
    k = k_ref[0]
    v = v_ref[0]
    sa = jnp.sum(s * a_ref[0], axis=-1, keepdims=True)
    s_new = s * jnp.exp(lw_ref[0]) + sa * b_ref[0] + v * k
    so_ref[0] = s_new
    y = jnp.sum(s_new * r, axis=-1, keepdims=True)
    mean = jnp.mean(y, axis=1, keepdims=True)
    d = y - mean
    var = jnp.mean(d * d, axis=1, keepdims=True)
    bonus = jnp.sum(r * k * rk_ref[...], axis=-1, keepdims=True) * v
    y_ref[0] = d * lax.rsqrt(var + GN_EPS) * gw_ref[...] + gb_ref[...] + bonus


def rwkv_step(state, r, lw, k, v, a, b, r_k, gn_w, gn_b):
    bsz, h, n, _ = state.shape
    row = lambda x: x.reshape(bsz, h, 1, n)
    col = lambda x: x.reshape(bsz, h, n, 1)
    row_spec = pl.BlockSpec((1, h, 1, n), lambda i: (i, 0, 0, 0))
    col_spec = pl.BlockSpec((1, h, n, 1), lambda i: (i, 0, 0, 0))
    st_spec = pl.BlockSpec((1, h, n, n), lambda i: (i, 0, 0, 0))
    prow = pl.BlockSpec((h, 1, n), lambda i: (0, 0, 0))
    pcol = pl.BlockSpec((h, n, 1), lambda i: (0, 0, 0))
    y, s_new = pl.pallas_call(
        _rwkv_step_kernel,
        grid=(bsz,),
        in_specs=[st_spec, row_spec, row_spec, row_spec, row_spec, row_spec, col_spec, prow, pcol, pcol],
        out_specs=[col_spec, st_spec],
        out_shape=[jax.ShapeDtypeStruct((bsz, h, n, 1), F32), jax.ShapeDtypeStruct((bsz, h, n, n), F32)],
        compiler_params=_cparams(("parallel",)),
        name="rwkv_step",
    )(state, row(r), row(lw), row(k), row(a), row(b), col(v),
      r_k.reshape(h, 1, n), gn_w.reshape(h, n, 1), gn_b.reshape(h, n, 1))
    return y.reshape(bsz, 1, h * n), s_new


def _moba_prompt_kernel(slope_ref, q_ref, k_ref, v_ref, o_ref, ko_ref, vo_ref, kb_ref, vb_ref, kmean_ref, rel_ref):
    h = pl.program_id(1)
    qi = pl.program_id(2)
    blk = q_ref.shape[1]
    t = k_ref.shape[1]
    n_blk = t // blk
    d = q_ref.shape[2]
    slope = slope_ref[h]
    scale = d ** -0.5

    @pl.when(qi == 0)
    def _():
        kf = k_ref[0]
        vf = v_ref[0]
        ko_ref[0, 0] = kf
        vo_ref[0, 0] = vf
        kb_ref[...] = kf.astype(BF16)
        vb_ref[...] = vf.astype(BF16)
        kmean_ref[...] = jnp.sum(kf.reshape(n_blk, blk, d), axis=1) * (1.0 / blk)
        rel = (lax.broadcasted_iota(jnp.int32, (blk, MOBA_KGROUP * blk), 0)
               - lax.broadcasted_iota(jnp.int32, (blk, MOBA_KGROUP * blk), 1)).astype(F32)
        rel_ref[...] = -slope * rel

    q = q_ref[0]
    gate = _dot_nt_x3(kmean_ref[...], q)
    nidx = lax.broadcasted_iota(jnp.int32, (n_blk, blk), 0).astype(F32)
    neg_inf = jnp.full((n_blk, blk), -jnp.inf, F32)
    vals = jnp.where(nidx < qi.astype(F32), gate, neg_inf)
    sel_bias = jnp.full((n_blk, blk), NEG_BIG, F32)
    for _ in range(MOBA_TOPK):
        mx = jnp.max(vals, axis=0, keepdims=True)
        is_max = jnp.logical_and(vals == mx, mx > -jnp.inf)
        first = jnp.min(jnp.where(is_max, nidx, float(n_blk)), axis=0, keepdims=True)
        pick = nidx == first
        sel_bias = jnp.where(pick, 0.0, sel_bias)
        vals = jnp.where(pick, neg_inf, vals)
    sel_bias = sel_bias.T.astype(BF16)

    subs = [slice(i * MOBA_ROWS, (i + 1) * MOBA_ROWS) for i in range(blk // MOBA_ROWS)]
    qbs = [q[sb, :].astype(BF16) for sb in subs]
    sel_bs = [sel_bias[sb, :] for sb in subs]

    def attend(kb, vb, biases, carry):
        ss = [_bdot_nt(qs, kb) * scale + bias for qs, bias in zip(qbs, biases)]
        m_new = [jnp.max(x, axis=1, keepdims=True) for x in ss]
        if carry is not None:
            m_new = [jnp.maximum(m, mn) for (m, _, _), mn in zip(carry, m_new)]
        ps = [jnp.exp(x - m) for x, m in zip(ss, m_new)]
        ls = [jnp.sum(p, axis=1, keepdims=True) for p in ps]
        pv = [jnp.dot(p.astype(BF16), vb, preferred_element_type=F32) for p in ps]
        if carry is None:
            return tuple(zip(m_new, ls, pv))
        alphas = [jnp.exp(m - mn) for (m, _, _), mn in zip(carry, m_new)]
        return tuple((mn, al * l + ln, al * acc + x)
                     for (_, l, acc), mn, al, ln, x in zip(carry, m_new, alphas, ls, pv))

    own = pl.multiple_of(qi * blk, blk)
    col = lax.broadcasted_iota(jnp.int32, (MOBA_ROWS, blk), 1)
    row = lax.broadcasted_iota(jnp.int32, (MOBA_ROWS, blk), 0)
    own_bias = [jnp.where(col <= row + sb.start, rel_ref[sb, 0:blk], NEG_BIG) for sb in subs]
    carry0 = attend(kb_ref[pl.ds(own, blk), :], vb_ref[pl.ds(own, blk), :], own_bias, None)
    gw = MOBA_KGROUP * blk
    group_block = (lax.broadcasted_iota(jnp.int32, (n_blk, gw), 0)
                   - lax.broadcasted_iota(jnp.int32, (n_blk, gw), 1) // blk)

    def body(j, carry):
        n0 = MOBA_KGROUP * j
        start = pl.multiple_of(n0 * blk, gw)
        onehot = (group_block == n0).astype(BF16)
        dist = ((qi - n0) * blk).astype(F32)
        biases = [rel_ref[sb, :] - slope * dist + jnp.dot(sel, onehot, preferred_element_type=F32)
                  for sb, sel in zip(subs, sel_bs)]
        return attend(kb_ref[pl.ds(start, gw), :], vb_ref[pl.ds(start, gw), :], biases, carry)

    final = lax.fori_loop(0, (qi + MOBA_KGROUP - 1) // MOBA_KGROUP, body, carry0)
    for sb, (_, l, acc) in zip(subs, final):
        o_ref[0, sb, :] = acc / l


def moba_prompt(z3, slopes, *, heads):
    bsz, t, _ = z3.shape
    d = B_HEAD_DIM
    blk = MOBA_BLOCK
    assert t % (MOBA_KGROUP * blk) == 0
    return pl.pallas_call(
        _moba_prompt_kernel,
        grid_spec=pltpu.PrefetchScalarGridSpec(
            num_scalar_prefetch=0,
            grid=(bsz, heads, t // blk),
            in_specs=[pl.BlockSpec(memory_space=pltpu.SMEM),
                      pl.BlockSpec((1, blk, d), lambda b, h, i: (b, i, h)),
                      pl.BlockSpec((1, t, d), lambda b, h, i: (b, 0, heads + h)),
                      pl.BlockSpec((1, t, d), lambda b, h, i: (b, 0, 2 * heads + h))],
            out_specs=[pl.BlockSpec((1, blk, d), lambda b, h, i: (b, i, h)),
                       pl.BlockSpec((1, 1, t, d), lambda b, h, i: (b, h, 0, 0)),
                       pl.BlockSpec((1, 1, t, d), lambda b, h, i: (b, h, 0, 0))],
            scratch_shapes=[pltpu.VMEM((t, d), BF16), pltpu.VMEM((t, d), BF16),
                            pltpu.VMEM((t // blk, d), F32), pltpu.VMEM((blk, MOBA_KGROUP * blk), F32)],
        ),
        out_shape=[jax.ShapeDtypeStruct((bsz, t, heads * d), F32),
                   jax.ShapeDtypeStruct((bsz, heads, t, d), F32),
                   jax.ShapeDtypeStruct((bsz, heads, t, d), F32)],
        compiler_params=_cparams(("parallel", "parallel", "arbitrary")),
        name="moba_prompt",
    )(slopes, z3, z3, z3)


def _paged_block_sum_kernel(pt_ref, *refs, pages_per_block):
    o_ref = refs[-1]
    for j in range(o_ref.shape[1]):
        acc = jnp.sum(refs[j * pages_per_block][...], axis=1)
        for s in range(1, pages_per_block):
            acc = acc + jnp.sum(refs[j * pages_per_block + s][...], axis=1)
        o_ref[0, j] = acc


def paged_block_sums(cache_k, page_table, *, pages_per_block, blocks_per_step):
    _, h, page, d = cache_k.shape
    bsz, n_pages = page_table.shape
    n_blk = n_pages // pages_per_block
    assert n_blk % blocks_per_step == 0
    pages_per_step = pages_per_block * blocks_per_step

    def page_spec(s):
        return pl.BlockSpec((None, h, page, d), lambda b, n, pt: (pt[b, n * pages_per_step + s], 0, 0, 0))

    return pl.pallas_call(
        functools.partial(_paged_block_sum_kernel, pages_per_block=pages_per_block),
        grid_spec=pltpu.PrefetchScalarGridSpec(
            num_scalar_prefetch=1,
            grid=(bsz, n_blk // blocks_per_step),
            in_specs=[page_spec(s) for s in range(pages_per_step)],
            out_specs=pl.BlockSpec((1, blocks_per_step, h, d), lambda b, n, pt: (b, n, 0, 0)),
        ),
        out_shape=jax.ShapeDtypeStruct((bsz, n_blk, h, d), F32),
        compiler_params=_cparams(("parallel", "arbitrary")),
        name="paged_block_sums",
    )(page_table, *([cache_k] * pages_per_step))


def _moba_select_kernel(ks_ref, q_ref, o_ref, *, block_len):
    n_blk, h = ks_ref.shape[1], ks_ref.shape[2]
    kmean = ks_ref[0] * (1.0 / block_len)
    gate = jnp.sum(kmean * q_ref[0], axis=-1, keepdims=True)
    nidx = lax.broadcasted_iota(jnp.int32, (n_blk, h, 1), 0).astype(F32)
    vals = gate
    for t in range(MOBA_TOPK):
        mx = jnp.max(vals, axis=0, keepdims=True)
        first = jnp.min(jnp.where(vals == mx, nidx, float(n_blk)), axis=0, keepdims=True)
        o_ref[0, t] = first[0].astype(jnp.int32)
        vals = jnp.where(nidx == first, -jnp.inf, vals)


def moba_select(ksum, q, *, block_len):
    bsz, n_blk, h, d = ksum.shape
    return pl.pallas_call(
        functools.partial(_moba_select_kernel, block_len=block_len),
        grid=(bsz,),
        in_specs=[pl.BlockSpec((1, n_blk, h, d), lambda b: (b, 0, 0, 0)),
                  pl.BlockSpec((1, h, d), lambda b: (b, 0, 0))],
        out_specs=pl.BlockSpec((1, MOBA_TOPK, h, 1), lambda b: (b, 0, 0, 0)),
        out_shape=jax.ShapeDtypeStruct((bsz, MOBA_TOPK, h, 1), jnp.int32),
        compiler_params=_cparams(("parallel",)),
        name="moba_select",
    )(ksum, q)


def _moba_decode_kernel(pt_ref, sel_ref, slope_ref, q_ref, kn_ref, vn_ref, ck_ref, cv_ref, o_ref,
                        kbuf, vbuf, sems, *, past_len, page, heads):
    b = pl.program_id(0)
    d = B_HEAD_DIM
    scale = d ** -0.5
    pages_per_block = MOBA_BLOCK // page
    n_slots = MOBA_TOPK * pages_per_block

    def page_copies(h, s):
        blk = sel_ref[b, s // pages_per_block, h]
        pg = pt_ref[b, blk * pages_per_block + s % pages_per_block]
        return (pltpu.make_async_copy(ck_ref.at[pg, h], kbuf.at[h, s], sems.at[0, h]),
                pltpu.make_async_copy(cv_ref.at[pg, h], vbuf.at[h, s], sems.at[1, h]))

    for h in range(heads):
        for s in range(n_slots):
            for cp in page_copies(h, s):
                cp.start()

    lane = lax.broadcasted_iota(jnp.int32, (1, page), 1)
    for h in range(heads):
        for s in range(n_slots):
            for cp in page_copies(h, s):
                cp.wait()
        sl = slice(h * d, (h + 1) * d)
        slope = slope_ref[h]
        q8 = jnp.broadcast_to(q_ref[0, :, sl], (8, d))
        logits = []
        for s in range(n_slots):
            blk = sel_ref[b, s // pages_per_block, h]
            kpos = blk * MOBA_BLOCK + (s % pages_per_block) * page + lane
            sc = _mm_nt(q8, kbuf[h, s], True)[0:1, :] * scale
            logits.append(sc - slope * (past_len - kpos).astype(F32))
        own = jnp.sum(q8[0:1, :] * kn_ref[0, :, sl], axis=1, keepdims=True) * scale
        m = own
        for sc in logits:
            m = jnp.maximum(m, jnp.max(sc, axis=1, keepdims=True))
        p_own = jnp.exp(own - m)
        l = p_own
        acc = p_own * vn_ref[0, :, sl]
        for s in range(n_slots):
            p = jnp.exp(logits[s] - m)
            l = l + jnp.sum(p, axis=1, keepdims=True)
            acc = acc + _mm(jnp.broadcast_to(p, (8, page)), vbuf[h, s], True)[0:1, :]
        o_ref[0, :, sl] = acc / l


def moba_decode(z3, cache_k, cache_v, page_table, sel, slopes, *, past_len):
    bsz = z3.shape[0]
    _, heads, page, d = cache_k.shape
    w = heads * d
    n_slots = MOBA_TOPK * (MOBA_BLOCK // page)
    row_spec = lambda col: pl.BlockSpec((1, 1, w), lambda b, pt, sel_: (b, 0, col))
    return pl.pallas_call(
        functools.partial(_moba_decode_kernel, past_len=past_len, page=page, heads=heads),
        grid_spec=pltpu.PrefetchScalarGridSpec(
            num_scalar_prefetch=2,
            grid=(bsz,),
            in_specs=[pl.BlockSpec(memory_space=pltpu.SMEM), row_spec(0), row_spec(1), row_spec(2),
                      pl.BlockSpec(memory_space=pl.ANY), pl.BlockSpec(memory_space=pl.ANY)],
            out_specs=pl.BlockSpec((1, 1, w), lambda b, pt, sel_: (b, 0, 0)),
            scratch_shapes=[pltpu.VMEM((heads, n_slots, page, d), F32),
                            pltpu.VMEM((heads, n_slots, page, d), F32),
                            pltpu.SemaphoreType.DMA((2, heads))],
        ),
        out_shape=jax.ShapeDtypeStruct((bsz, 1, w), F32),
        compiler_params=_cparams(("arbitrary",)),
        name="moba_decode",
    )(page_table, sel, slopes, z3, z3, z3, cache_k, cache_v)


def _mem_attn_kernel(q_ref, k_ref, v_ref, o_ref, *, precise):
    tq = q_ref.shape[1]
    scale = C_HEAD_DIM ** -0.5
    rows = max(tq, 8)
    for h in range(C_HEADS):
        sl = slice(h * C_HEAD_DIM, (h + 1) * C_HEAD_DIM)
        q = jnp.broadcast_to(q_ref[0, :, sl], (rows, C_HEAD_DIM)) if tq < 8 else q_ref[0, :, sl]
        s = _mm_nt(q, k_ref[0, :, sl], precise) * scale
        m = jnp.max(s, axis=1, keepdims=True)
        p = jnp.exp(s - m)
        l = jnp.sum(p, axis=1, keepdims=True)
        o = _mm(p, v_ref[0, :, sl], precise) / l
        o_ref[0, :, sl] = o[0:tq, :]


def mem_attention(z3, mem_k3, mem_v3, *, q_col, k_col, v_col, tq, precise):
    bsz, t, _ = z3.shape
    mlen = mem_k3.shape[1]
    return pl.pallas_call(
        functools.partial(_mem_attn_kernel, precise=precise),
        grid=(bsz, t // tq),
        in_specs=[pl.BlockSpec((1, tq, C_WIDTH), lambda b, i: (b, i, q_col)),
                  pl.BlockSpec((1, mlen, C_WIDTH), lambda b, i: (b, 0, k_col)),
                  pl.BlockSpec((1, mlen, C_WIDTH), lambda b, i: (b, 0, v_col))],
        out_specs=pl.BlockSpec((1, tq, C_WIDTH), lambda b, i: (b, i, 0)),
        out_shape=jax.ShapeDtypeStruct((bsz, t, C_WIDTH), F32),
        compiler_params=_cparams(("parallel", "parallel")),
        name="mem_attention",
    )(z3, mem_k3, mem_v3)


def _out_proj_kernel(ymix_ref, ymem_ref, g0_ref, g1_ref, w_ref, g_ref, x_ref, o_ref, *, precise):
    half = g0_ref.shape[1]
    wmix = ymix_ref.shape[1]
    silu = lambda t: t * _sigmoid(t)
    g1 = g1_ref[...]
    u0 = ymix_ref[:, 0:half] * silu(g0_ref[...])
    u1 = ymix_ref[:, half:wmix] * silu(g1[:, 0:wmix - half])
    u2 = ymem_ref[...] * silu(g1[:, wmix - half:])
    y = _mm(u0, w_ref[0:half, :], precise)
    y = y + _mm(u1, w_ref[half:wmix, :], precise)
    y = y + _mm(u2, w_ref[wmix:, :], precise)
    o_ref[...] = x_ref[...] + _rms_scale(y, g_ref[...])


def out_proj(ymix, ymem, z, w_out, g, x, *, gate_col, tm):
    m, dm = x.shape
    wmix = ymix.shape[1]
    half = 1024
    return pl.pallas_call(
        functools.partial(_out_proj_kernel, precise=w_out.dtype == F32),
        grid=(m // tm,),
        in_specs=[pl.BlockSpec((tm, wmix), lambda i: (i, 0)),
                  pl.BlockSpec((tm, C_WIDTH), lambda i: (i, 0)),
                  pl.BlockSpec((tm, half), lambda i: (i, gate_col)),
                  pl.BlockSpec((tm, half), lambda i: (i, gate_col + 1)),
                  pl.BlockSpec(w_out.shape, lambda i: (0, 0)),
                  pl.BlockSpec((1, dm), lambda i: (0, 0)),
                  pl.BlockSpec((tm, dm), lambda i: (i, 0))],
        out_specs=pl.BlockSpec((tm, dm), lambda i: (i, 0)),
        out_shape=jax.ShapeDtypeStruct((m, dm), F32),
        compiler_params=_cparams(("parallel",)),
        name="out_proj",
    )(ymix, ymem, z, z, w_out, g, x)


def _prep_rwkv_weights(w_in, mu, decay_b, iclr_b, vres_b, a_width):
    rkv = 3 * a_width
    n_lora = mu.shape[0] - rkv
    pad = LORA_PAD - n_lora
    dm = w_in.shape[0]
    w = jnp.concatenate([w_in[:, :rkv], w_in[:, rkv + n_lora:], w_in[:, rkv:rkv + n_lora],
                         jnp.zeros((dm, pad), w_in.dtype)], axis=1)
    mu_r = mu[:rkv].reshape(1, rkv)
    mu_l = jnp.pad(mu[rkv:], (0, pad)).reshape(1, LORA_PAD)

    def pad_rows(m, start):
        return jnp.pad(m, ((start, LORA_PAD - start - m.shape[0]), (0, 0)))

    wd = pad_rows(decay_b, 0)
    wi = pad_rows(iclr_b, DECAY_LORA)
    wv = None if vres_b is None else pad_rows(vres_b, DECAY_LORA + ICLR_LORA)
    return w, mu_r, mu_l, wd, wi, wv


def kernel(x_prompt, x_sample, state_wkv, state_shift, cache_k_l1, cache_v_l1, cache_k_l3, cache_v_l3, cache_mem_k, cache_mem_v, page_table, mem_prompt, norm_pre, norm_post, w_in_l0, w_in_l1, w_in_l2, w_in_l3, w_mem_kv, w_out, mu_l0, mu_l2, rwkv_decay_b, rwkv_decay_0, rwkv_iclr_b, rwkv_iclr_0, rwkv_k_k, rwkv_k_a, rwkv_r_k, rwkv_gn_w, rwkv_gn_b, vres_b_l2, vres_0_l2):
    depth = norm_pre.shape[0]
    bp, seq, dm = x_prompt.shape
    bs, dec_seq, _ = x_sample.shape
    assert dec_seq == 1
    a_heads = state_wkv.shape[2]
    a_width = a_heads * A_HEAD_DIM
    n_pool, page, b_heads, _ = cache_k_l1.shape
    b_width = b_heads * B_HEAD_DIM
    mem_len = mem_prompt.shape[1]
    past_len = page_table.shape[1] * page
    assert past_len % MOBA_BLOCK == 0 and past_len // MOBA_BLOCK >= MOBA_TOPK and MOBA_BLOCK % page == 0
    assert a_width == b_width
    mix_w = a_width
    qmem_col = 3 * mix_w // C_WIDTH
    gate_col = (3 * mix_w + C_WIDTH) // 1024
    lora_col = (3 * mix_w + C_WIDTH + dm) // LORA_PAD

    zeros8 = jnp.zeros((a_width,), F32)
    rw = []
    for j, (w_in, mu, vb, v0) in enumerate(((w_in_l0, mu_l0, None, None), (w_in_l2, mu_l2, vres_b_l2, vres_0_l2))):
        w, mu_r, mu_l, wd, wi, wv = _prep_rwkv_weights(w_in, mu, rwkv_decay_b[j], rwkv_iclr_b[j], vb, a_width)
        pvec = jnp.stack([rwkv_decay_0[j], rwkv_iclr_0[j], zeros8 if v0 is None else v0, rwkv_k_k[j], rwkv_k_a[j],
                          zeros8, zeros8, zeros8])
        pvec2 = jnp.stack([rwkv_r_k[j].reshape(-1), rwkv_gn_w[j], rwkv_gn_b[j]] + [zeros8] * 5)
        lora = {True: (wd.astype(BF16), wi.astype(BF16), None if wv is None else wv.astype(BF16)),
                False: (wd, wi, wv)}
        rw.append(dict(w={True: w.astype(BF16), False: w}, lora=lora, mu_r=mu_r, mu_l=mu_l, pvec=pvec, pvec2=pvec2))
    w_moba = [{True: w.astype(BF16), False: w} for w in (w_in_l1, w_in_l3)]
    w_out_p = {True: w_out.astype(BF16), False: w_out}
    w_mem_all = jnp.transpose(w_mem_kv, (1, 0, 2)).reshape(dm, depth * 2 * C_WIDTH).astype(BF16)
    slopes = jnp.asarray(alibi_slopes(b_heads))
    ones_g = jnp.ones((1, dm), F32)
    caches = [tuple(jnp.transpose(c, (0, 2, 1, 3)) for c in pair)
              for pair in ((cache_k_l1, cache_v_l1), (cache_k_l3, cache_v_l3))]

    mem_rows = mem_prompt.reshape(bp * mem_len, dm)
    memkv = norm_matmul(mem_rows, ones_g, w_mem_all, normalize=False, tm=bp * mem_len, tn=512)
    memkv3 = memkv.reshape(bp, mem_len, depth * 2 * C_WIDTH)
    memkv5 = memkv.reshape(bp, mem_len, depth, 2, C_HEADS, C_HEAD_DIM)
    p_mem_k = jnp.transpose(memkv5[:, :, :, 0], (2, 0, 1, 3, 4))
    p_mem_v = jnp.transpose(memkv5[:, :, :, 1], (2, 0, 1, 3, 4))
    cache_mem_k3 = cache_mem_k.reshape(depth, bs, mem_len, C_WIDTH)
    cache_mem_v3 = cache_mem_v.reshape(depth, bs, mem_len, C_WIDTH)

    def trunk(x3, is_prompt):
        bsz, t, _ = x3.shape
        m = bsz * t
        x = x3.reshape(m, dm)
        tm_a = 1024 if m % 1024 == 0 else m
        tm_f = 256 if m % 256 == 0 else m
        v_first = None
        wkv_out, shift_out, kv_out = [], [], []
        for i in range(depth):
            j = i // 2
            g_pre = norm_pre[i].reshape(1, dm)
            if i % 2 == 0:
                p = rw[j]
                z = norm_matmul(x, g_pre, p["w"][is_prompt], normalize=True, tm=tm_a, tn=512)
                z3 = z.reshape(bsz, t, -1)
                if is_prompt:
                    zprev3 = jnp.zeros((bsz, 1, z.shape[1]), F32)
                else:
                    zprev3 = norm_matmul(state_shift[j], ones_g, p["w"][is_prompt], normalize=False, tm=bsz, tn=512
                                         ).reshape(bsz, 1, -1)
                wd, wi, wv = p["lora"][is_prompt]
                r, lw, k, v, a, b = rwkv_prep(z3, zprev3, p["mu_r"], p["mu_l"], wd, wi, wv, p["pvec"],
                                              v_first, tt=min(t, 128), width=a_width, lora_col=lora_col)
                if v_first is None:
                    v_first = v
                if is_prompt:
                    y_mix, h_bd = rwkv_chunk_scan(r, lw, k, v, a, b, p["pvec2"], pairs_per_step=a_heads // 2)
                    hb = h_bd.reshape(bsz, a_heads // 2, 2, A_HEAD_DIM, 2, A_HEAD_DIM)
                    s_new = jnp.stack([hb[:, :, 0, :, 0, :], hb[:, :, 1, :, 1, :]], axis=2)
                    s_new = jnp.swapaxes(s_new.reshape(bsz, a_heads, A_HEAD_DIM, A_HEAD_DIM), -1, -2)
                else:
                    y_mix, s_new = rwkv_step(state_wkv[j], r, lw, k, v, a, b, rwkv_r_k[j], rwkv_gn_w[j], rwkv_gn_b[j])
                wkv_out.append(s_new)
                shift_out.append(rmsnorm_rows(x3[:, -1], g_pre))
            else:
                z = norm_matmul(x, g_pre, w_moba[j][is_prompt], normalize=True, tm=tm_a, tn=512)
                z3 = z.reshape(bsz, t, -1)
                if is_prompt:
                    y_mix, k_hm, v_hm = moba_prompt(z3, slopes, heads=b_heads)
                    kv_out.append((jnp.transpose(k_hm, (0, 2, 1, 3)), jnp.transpose(v_hm, (0, 2, 1, 3))))
                else:
                    ck, cv = caches[j]
                    ksum = paged_block_sums(ck, page_table, pages_per_block=MOBA_BLOCK // page, blocks_per_step=4)
                    sel = moba_select(ksum, z3[:, 0, :b_width].reshape(bsz, b_heads, B_HEAD_DIM),
                                      block_len=MOBA_BLOCK)
                    y_mix = moba_decode(z3, ck, cv, page_table, sel[..., 0], slopes, past_len=past_len)
                    kv_out.append((z3[:, :, b_width:2 * b_width].reshape(bsz, t, b_heads, B_HEAD_DIM),
                                   z3[:, :, 2 * b_width:3 * b_width].reshape(bsz, t, b_heads, B_HEAD_DIM)))
            if is_prompt:
                y_mem = mem_attention(z3, memkv3, memkv3, q_col=qmem_col, k_col=2 * i, v_col=2 * i + 1,
                                      tq=min(t, 512), precise=False)
            else:
                y_mem = mem_attention(z3, cache_mem_k3[i], cache_mem_v3[i], q_col=qmem_col, k_col=0, v_col=0, tq=t,
                                      precise=True)
            x = out_proj(y_mix.reshape(m, mix_w), y_mem.reshape(m, C_WIDTH), z, w_out_p[is_prompt][i],
                         norm_post[i].reshape(1, dm), x, gate_col=gate_col, tm=tm_f)
            x3 = x.reshape(bsz, t, dm)
        return x3, jnp.stack(wkv_out), jnp.stack(shift_out), kv_out

    y_prompt, p_wkv, p_shift, p_kv = trunk(x_prompt, True)
    y_sample, s_wkv, s_shift, s_kv = trunk(x_sample, False)
    (p_k_l1, p_v_l1), (p_k_l3, p_v_l3) = p_kv
    (s_k_l1, s_v_l1), (s_k_l3, s_v_l3) = s_kv
    return (y_prompt, y_sample, p_wkv, p_shift, p_k_l1, p_v_l1, p_k_l3, p_v_l3, p_mem_k, p_mem_v,
            s_wkv, s_shift, s_k_l1, s_v_l1, s_k_l3, s_v_l3)
```

```python
import functools
import math

import numpy as np
import jax
import jax.numpy as jnp
from jax import lax
from jax.experimental import pallas as pl
from jax.experimental.pallas import tpu as pltpu

F32 = jnp.float32
BF16 = jnp.bfloat16

RMS_EPS = 1e-6
GN_EPS = 64e-5
A_HEAD_DIM = 64
B_HEAD_DIM = 128
C_HEADS = 4
C_HEAD_DIM = 128
C_WIDTH = C_HEADS * C_HEAD_DIM
MOBA_BLOCK = 256
MOBA_TOPK = 3
DECAY_LORA = 96
ICLR_LORA = 96
VRES_LORA = 64

LANES = 128
RWKV_CHUNK = 64
LORA_PAD = 512
NEG_BIG = -1e30
MOBA_ROWS = 128
MOBA_KGROUP = 2
MOBA_HEADS_PER_STEP = 2
VMEM_LIMIT = 48 * 1024 * 1024


def _cparams(sem):
    return pltpu.CompilerParams(dimension_semantics=sem, vmem_limit_bytes=VMEM_LIMIT)


def alibi_slopes(n):
    def pow2_slopes(m):
        start = 2.0 ** (-(2.0 ** -(math.log2(m) - 3)))
        return [start ** (i + 1) for i in range(m)]
    if math.log2(n).is_integer():
        s = pow2_slopes(n)
    else:
        c = 2 ** math.floor(math.log2(n))
        s = pow2_slopes(c) + pow2_slopes(2 * c)[0::2][: n - c]
    return np.asarray(s, dtype=np.float32)


def _bdot(a, b):
    return jnp.dot(a.astype(BF16), b.astype(BF16), preferred_element_type=F32)


def _bdot_nt(a, b):
    return lax.dot_general(a.astype(BF16), b.astype(BF16), (((1,), (1,)), ((), ())),
                           preferred_element_type=F32)


def _mm(a, b, precise):
    if precise:
        return jnp.dot(a, b, preferred_element_type=F32, precision=lax.Precision.HIGHEST)
    return _bdot(a, b)


def _mm_nt(a, b, precise):
    if precise:
        return lax.dot_general(a, b, (((1,), (1,)), ((), ())), preferred_element_type=F32,
                               precision=lax.Precision.HIGHEST)
    return _bdot_nt(a, b)


def _split3(x):
    hi = x.astype(BF16)
    r1 = x - hi.astype(F32)
    mid = r1.astype(BF16)
    lo = (r1 - mid.astype(F32)).astype(BF16)
    return hi, mid, lo


def _dot_rhs01(x, q01):
    hi, mid, lo = _split3(x)
    acc = jnp.dot(lo, q01, preferred_element_type=F32)
    acc = acc + jnp.dot(mid, q01, preferred_element_type=F32)
    return acc + jnp.dot(hi, q01, preferred_element_type=F32)


def _dot_lhs01(t01, x):
    hi, mid, lo = _split3(x)
    acc = jnp.dot(t01, lo, preferred_element_type=F32)
    acc = acc + jnp.dot(t01, mid, preferred_element_type=F32)
    return acc + jnp.dot(t01, hi, preferred_element_type=F32)


def _dot_nt_x3(a, b):
    ah = a.astype(BF16)
    al = (a - ah.astype(F32)).astype(BF16)
    bh = b.astype(BF16)
    bl = (b - bh.astype(F32)).astype(BF16)
    dn = (((1,), (1,)), ((), ()))
    acc = lax.dot_general(al, bh, dn, preferred_element_type=F32)
    acc = acc + lax.dot_general(ah, bl, dn, preferred_element_type=F32)
    return acc + lax.dot_general(ah, bh, dn, preferred_element_type=F32)


def _head_sum_matrix(width, head):
    r = lax.broadcasted_iota(jnp.int32, (width, width), 0) // head
    c = lax.broadcasted_iota(jnp.int32, (width, width), 1) // head
    return (r == c).astype(BF16)


def _sigmoid(x):
    return 1.0 / (1.0 + jnp.exp(-x))


def _softplus(x):
    return jnp.maximum(x, 0.0) + jnp.log(1.0 + jnp.exp(-jnp.abs(x)))


def _rms_scale(x, g):
    ms = jnp.mean(x * x, axis=-1, keepdims=True)
    return x * lax.rsqrt(ms + RMS_EPS) * g


def _norm_matmul_kernel(x_ref, g_ref, w_ref, o_ref, h_ref, *, normalize, row_block, precise):
    @pl.when(pl.program_id(1) == 0)
    def _():
        tm = x_ref.shape[0]
        for s in range(0, tm, row_block):
            x = x_ref[s:s + row_block, :]
            if normalize:
                x = _rms_scale(x, g_ref[...])
            h_ref[s:s + row_block, :] = x.astype(h_ref.dtype)
    o_ref[...] = _mm(h_ref[...], w_ref[...], precise)


def norm_matmul(x, g, w, *, normalize, tm, tn):
    m, k = x.shape
    n = w.shape[1]
    assert m % tm == 0 and n % tn == 0, (m, tm, n, tn)
    row_block = min(tm, 256)
    precise = w.dtype == F32
    return pl.pallas_call(
        functools.partial(_norm_matmul_kernel, normalize=normalize, row_block=row_block, precise=precise),
        grid=(m // tm, n // tn),
        in_specs=[pl.BlockSpec((tm, k), lambda i, j: (i, 0)),
                  pl.BlockSpec((1, k), lambda i, j: (0, 0)),
                  pl.BlockSpec((k, tn), lambda i, j: (0, j))],
        out_specs=pl.BlockSpec((tm, tn), lambda i, j: (i, j)),
        out_shape=jax.ShapeDtypeStruct((m, n), F32),
        scratch_shapes=[pltpu.VMEM((tm, k), w.dtype)],
        compiler_params=_cparams(("parallel", "arbitrary")),
        name="norm_matmul" if normalize else "plain_matmul",
    )(x, g, w)


def _rmsnorm_rows_kernel(x_ref, g_ref, o_ref):
    o_ref[...] = _rms_scale(x_ref[...], g_ref[...])


def rmsnorm_rows(x, g):
    return pl.pallas_call(
        _rmsnorm_rows_kernel,
        out_shape=jax.ShapeDtypeStruct(x.shape, F32),
        name="rmsnorm_rows",
    )(x, g)


def _rwkv_prep_kernel(*refs, has_vres, width, precise):
    if has_vres:
        (zr_ref, zl_ref, pfr_ref, pfl_ref, mur_ref, mul_ref, wd_ref, wi_ref, wv_ref, pv_ref, vf_ref,
         r_o, lw_o, k_o, v_o, a_o, b_o, carry_r, carry_l) = refs
    else:
        (zr_ref, zl_ref, pfr_ref, pfl_ref, mur_ref, mul_ref, wd_ref, wi_ref, pv_ref,
         r_o, lw_o, k_o, v_o, a_o, b_o, carry_r, carry_l) = refs
        wv_ref = vf_ref = None
    tt = zr_ref.shape[1]

    @pl.when(pl.program_id(1) == 0)
    def _():
        carry_r[...] = pfr_ref[0]
        carry_l[...] = pfl_ref[0]

    if tt > 1:
        first_row = lax.broadcasted_iota(jnp.int32, (tt, 1), 0) == 0

    def shift_mix(z, carry, mu):
        if tt > 1:
            prev = jnp.where(first_row, carry, pltpu.roll(z, 1, 0))
        else:
            prev = carry
        return z + mu * (prev - z)

    zl = zl_ref[0]
    ls = shift_mix(zl, carry_l[...], mul_ref[...])
    carry_l[...] = zl[tt - 1:tt, :]

    decay_0 = pv_ref[0:1, :]
    iclr_0 = pv_ref[1:2, :]
    vres_0 = pv_ref[2:3, :]
    k_k = pv_ref[3:4, :]
    k_a = pv_ref[4:5, :]

    log_w = -_softplus(-(decay_0 + _mm(jnp.tanh(ls), wd_ref[...], precise))) - 0.5
    lw_o[0] = -jnp.exp(log_w)
    rate = _sigmoid(iclr_0 + _mm(ls, wi_ref[...], precise))

    def seg(i):
        z = zr_ref[0, :, i * width:(i + 1) * width]
        out = shift_mix(z, carry_r[:, i * width:(i + 1) * width], mur_ref[:, i * width:(i + 1) * width])
        carry_r[:, i * width:(i + 1) * width] = z[tt - 1:tt, :]
        return out

    r_o[0] = seg(0)
    k = seg(1)
    v = seg(2)
    if has_vres:
        v = v + (vf_ref[0] - v) * _sigmoid(vres_0 + _mm(ls, wv_ref[...], precise))
    v_o[0] = v

    q01 = _head_sum_matrix(LANES, A_HEAD_DIM)
    kk = k * k_k
    for p in range(width // LANES):
        sl = slice(p * LANES, (p + 1) * LANES)
        kp = kk[:, sl]
        norm = jnp.sqrt(_dot_rhs01(kp * kp, q01))
        kp = kp / jnp.maximum(norm, 1e-12)
        a_o[0, :, sl] = -kp
        b_o[0, :, sl] = kp * rate[:, sl]
    k_o[0] = k * (1.0 + (rate - 1.0) * k_a)


def rwkv_prep(z3, zprev3, mu_r, mu_l, wd, wi, wv, pvec, v_first, *, tt, width, lora_col):
    b, t, _ = z3.shape
    has_vres = wv is not None
    rkv = 3 * width
    in_specs = [
        pl.BlockSpec((1, tt, rkv), lambda i, j: (i, j, 0)),
        pl.BlockSpec((1, tt, LORA_PAD), lambda i, j: (i, j, lora_col)),
        pl.BlockSpec((1, 1, rkv), lambda i, j: (i, 0, 0)),
        pl.BlockSpec((1, 1, LORA_PAD), lambda i, j: (i, 0, lora_col)),
        pl.BlockSpec((1, rkv), lambda i, j: (0, 0)),
        pl.BlockSpec((1, LORA_PAD), lambda i, j: (0, 0)),
        pl.BlockSpec((LORA_PAD, width), lambda i, j: (0, 0)),
        pl.BlockSpec((LORA_PAD, width), lambda i, j: (0, 0)),
    ]
    args = [z3, z3, zprev3, zprev3, mu_r, mu_l, wd, wi]
    if has_vres:
        in_specs.append(pl.BlockSpec((LORA_PAD, width), lambda i, j: (0, 0)))
        args.append(wv)
    in_specs.append(pl.BlockSpec((8, width), lambda i, j: (0, 0)))
    args.append(pvec)
    if has_vres:
        in_specs.append(pl.BlockSpec((1, tt, width), lambda i, j: (i, j, 0)))
        args.append(v_first)
    out_spec = pl.BlockSpec((1, tt, width), lambda i, j: (i, j, 0))
    out_sds = jax.ShapeDtypeStruct((b, t, width), F32)
    return pl.pallas_call(
        functools.partial(_rwkv_prep_kernel, has_vres=has_vres, width=width, precise=wd.dtype == F32),
        grid=(b, t // tt),
        in_specs=in_specs,
        out_specs=[out_spec] * 6,
        out_shape=[out_sds] * 6,
        scratch_shapes=[pltpu.VMEM((1, rkv), F32), pltpu.VMEM((1, LORA_PAD), F32)],
        compiler_params=_cparams(("parallel", "arbitrary")),
        name="rwkv_prep",
    )(*args)


def _stack_heads(x, lane_is_head0):
    zero = jnp.zeros_like(x)
    return jnp.concatenate([jnp.where(lane_is_head0, x, zero), jnp.where(lane_is_head0, zero, x)], axis=0)


def _rwkv_group_chunk(rs, lws, ks, vs, as_, bs, hs, c):
    L = rs[0].shape[0]
    n2 = 2 * L
    each = lambda f, *xs: [f(*t) for t in zip(*xs)]
    st = functools.partial(_stack_heads, lane_is_head0=c["head0"])
    zero = jnp.zeros((n2, n2), F32)
    cum = [_dot_lhs01(c["tri"], lw) for lw in lws]
    tot = [x[L - 1:L, :] for x in cum]
    a_st = each(lambda a, cm, lw: st(a * jnp.exp(cm - lw)), as_, cum, lws)
    r_st = each(lambda r, cm: st(r * jnp.exp(cm)), rs, cum)
    v_st = [st(v) for v in vs]
    g_inv = [jnp.exp(-cm) for cm in cum]
    rhs = each(lambda b, k, g: jnp.concatenate([st(b * g), st(k * g)], axis=0), bs, ks, g_inv)
    gram = each(lambda a, r, x: _bdot_nt(jnp.concatenate([a, r], axis=0), x), a_st, r_st, rhs)
    a_ab = [jnp.where(c["strict"], g[0:n2, 0:n2], zero) for g in gram]
    a_ak = [jnp.where(c["strict"], g[0:n2, n2:2 * n2], zero) for g in gram]
    a_rb = [jnp.where(c["incl"], g[n2:2 * n2, 0:n2], zero) for g in gram]
    a_rk = [jnp.where(c["incl"], g[n2:2 * n2, n2:2 * n2], zero) for g in gram]
    pw = a_ab
    inv = [c["eye"] + x for x in a_ab]
    n = 1
    while 2 * n < L:
        pw = [_bdot(x, x) for x in pw]
        inv = each(lambda i, p: i + _bdot(i, p), inv, pw)
        n *= 2
    a_p = each(_bdot, inv, a_st)
    av = each(_bdot, a_ak, v_st)
    u0 = each(_bdot, inv, av)
    uv = each(lambda u, v: jnp.concatenate([u, v], axis=0), u0, v_st)
    g_end = each(lambda t, cm: jnp.exp(t - cm), tot, cum)
    bk_end_t = each(lambda b, k, g: jnp.concatenate([st(b * g), st(k * g)], axis=0).T, bs, ks, g_end)
    m_bd = each(lambda t, bt, ap: c["eye"] * jnp.exp(t) + _bdot(bt[:, 0:n2], ap), tot, bk_end_t, a_p)
    c_bd = each(_bdot, bk_end_t, uv)
    r_p = each(lambda r, arb, ap: r + _bdot(arb, ap), r_st, a_rb, a_p)
    y0 = each(lambda arb, ark, x: _bdot(jnp.concatenate([arb, ark], axis=1), x), a_rb, a_rk, uv)
    y_st = each(lambda rp, h, y: _bdot(rp, h) + y, r_p, hs, y0)
    h_new = each(lambda m, h, cc: _bdot(m, h) + cc, m_bd, hs, c_bd)
    return [y[0:L, :] + y[L:n2, :] for y in y_st], h_new


def _rwkv_chunk_kernel(r_ref, lw_ref, k_ref, v_ref, a_ref, b_ref, pv_ref, y_ref, hout_ref, h_ref, *, pairs):
    L = r_ref.shape[1]
    n2 = 2 * L

    @pl.when(pl.program_id(2) == 0)
    def _():
        h_ref[...] = jnp.zeros_like(h_ref)

    ri = lax.broadcasted_iota(jnp.int32, (n2, n2), 0)
    ci = lax.broadcasted_iota(jnp.int32, (n2, n2), 1)
    consts = {
        "tri": (lax.broadcasted_iota(jnp.int32, (L, L), 0) >= lax.broadcasted_iota(jnp.int32, (L, L), 1)).astype(BF16),
        "head0": lax.broadcasted_iota(jnp.int32, (1, LANES), 1) < A_HEAD_DIM,
        "strict": (ri % L) > (ci % L),
        "incl": (ri % L) >= (ci % L),
        "eye": (ri == ci).astype(F32),
    }
    sls = [slice(p * LANES, (p + 1) * LANES) for p in range(pairs)]
    load = lambda ref: [ref[0, :, sl] for sl in sls]
    rs, ks, vs = load(r_ref), load(k_ref), load(v_ref)
    ys, h_new = _rwkv_group_chunk(rs, load(lw_ref), ks, vs, load(a_ref), load(b_ref),
                                  [h_ref[p] for p in range(pairs)], consts)
    for p in range(pairs):
        h_ref[p] = h_new[p]
    q01 = _head_sum_matrix(LANES, A_HEAD_DIM)
    inv_n = 1.0 / A_HEAD_DIM
    mean = [_dot_rhs01(y, q01) * inv_n for y in ys]
    dev = [y - m for y, m in zip(ys, mean)]
    var = [_dot_rhs01(d * d, q01) * inv_n for d in dev]
    bonus = [_dot_rhs01(r * k * pv_ref[0:1, sl], q01) * v for r, k, v, sl in zip(rs, ks, vs, sls)]
    for p, sl in enumerate(sls):
        y_ref[0, :, sl] = dev[p] * lax.rsqrt(var[p] + GN_EPS) * pv_ref[1:2, sl] + pv_ref[2:3, sl] + bonus[p]

    @pl.when(pl.program_id(2) == pl.num_programs(2) - 1)
    def _():
        hout_ref[0] = h_ref[...]


def rwkv_chunk_scan(r, lw, k, v, a, b, pvec, *, pairs_per_step):
    bsz, t, w = r.shape
    n_pairs = w // LANES
    assert n_pairs % pairs_per_step == 0 and t % RWKV_CHUNK == 0
    gw = pairs_per_step * LANES
    io_spec = pl.BlockSpec((1, RWKV_CHUNK, gw), lambda i, g, c: (i, c, g))
    return pl.pallas_call(
        functools.partial(_rwkv_chunk_kernel, pairs=pairs_per_step),
        grid=(bsz, n_pairs // pairs_per_step, t // RWKV_CHUNK),
        in_specs=[io_spec] * 6 + [pl.BlockSpec((8, gw), lambda i, g, c: (0, g))],
        out_specs=[io_spec, pl.BlockSpec((1, pairs_per_step, LANES, LANES), lambda i, g, c: (i, g, 0, 0))],
        out_shape=[jax.ShapeDtypeStruct((bsz, t, w), F32),
                   jax.ShapeDtypeStruct((bsz, n_pairs, LANES, LANES), F32)],
        scratch_shapes=[pltpu.VMEM((pairs_per_step, LANES, LANES), F32)],
        compiler_params=_cparams(("parallel", "parallel", "arbitrary")),
        name="rwkv_chunk_scan",
    )(r, lw, k, v, a, b, pvec)


def _rwkv_step_kernel(s_ref, r_ref, lw_ref, k_ref, a_ref, b_ref, v_ref, rk_ref, gw_ref, gb_ref, y_ref, so_ref):
    s = s_ref[0]
    r = r_ref[0]
    k = k_ref[0]
    v = v_ref[0]
    sa = jnp.sum(s * a_ref[0], axis=-1, keepdims=True)
    s_new = s * jnp.exp(lw_ref[0]) + sa * b_ref[0] + v * k
    so_ref[0] = s_new
    y = jnp.sum(s_new * r, axis=-1, keepdims=True)
    mean = jnp.mean(y, axis=1, keepdims=True)
    d = y - mean
    var = jnp.mean(d * d, axis=1, keepdims=True)
    bonus = jnp.sum(r * k * rk_ref[...], axis=-1, keepdims=True) * v
    y_ref[0] = d * lax.rsqrt(var + GN_EPS) * gw_ref[...] + gb_ref[...] + bonus


def rwkv_step(state, r, lw, k, v, a, b, r_k, gn_w, gn_b):
    bsz, h, n, _ = state.shape
    row = lambda x: x.reshape(bsz, h, 1, n)
    col = lambda x: x.reshape(bsz, h, n, 1)
    row_spec = pl.BlockSpec((1, h, 1, n), lambda i: (i, 0, 0, 0))
    col_spec = pl.BlockSpec((1, h, n, 1), lambda i: (i, 0, 0, 0))
    st_spec = pl.BlockSpec((1, h, n, n), lambda i: (i, 0, 0, 0))
    prow = pl.BlockSpec((h, 1, n), lambda i: (0, 0, 0))
    pcol = pl.BlockSpec((h, n, 1), lambda i: (0, 0, 0))
    y, s_new = pl.pallas_call(
        _rwkv_step_kernel,
        grid=(bsz,),
        in_specs=[st_spec, row_spec, row_spec, row_spec, row_spec, row_spec, col_spec, prow, pcol, pcol],
        out_specs=[col_spec, st_spec],
        out_shape=[jax.ShapeDtypeStruct((bsz, h, n, 1), F32), jax.ShapeDtypeStruct((bsz, h, n, n), F32)],
        compiler_params=_cparams(("parallel",)),
        name="rwkv_step",
    )(state, row(r), row(lw), row(k), row(a), row(b), col(v),
      r_k.reshape(h, 1, n), gn_w.reshape(h, n, 1), gn_b.reshape(h, n, 1))
    return y.reshape(bsz, 1, h * n), s_new


def _moba_prompt_kernel(slope_ref, q_ref, k_ref, v_ref, o_ref, ko_ref, vo_ref, kb_ref, vb_ref, kmean_ref, rel_ref):
    g = pl.program_id(1)
    qi = pl.program_id(2)
    blk = q_ref.shape[1]
    t = k_ref.shape[1]
    n_blk = t // blk
    d = B_HEAD_DIM
    hg = q_ref.shape[2] // d
    scale = d ** -0.5
    hsl = [slice(hh * d, (hh + 1) * d) for hh in range(hg)]
    slopes = [slope_ref[g * hg + hh] for hh in range(hg)]
    gw = MOBA_KGROUP * blk

    @pl.when(qi == 0)
    def _():
        rel = (lax.broadcasted_iota(jnp.int32, (blk, gw), 0)
               - lax.broadcasted_iota(jnp.int32, (blk, gw), 1)).astype(F32)
        for hh in range(hg):
            kf = k_ref[0, :, hsl[hh]]
            vf = v_ref[0, :, hsl[hh]]
            ko_ref[0, hh] = kf
            vo_ref[0, hh] = vf
            kb_ref[:, hsl[hh]] = kf.astype(BF16)
            vb_ref[:, hsl[hh]] = vf.astype(BF16)
            kmean_ref[:, hsl[hh]] = jnp.sum(kf.reshape(n_blk, blk, d), axis=1) * (1.0 / blk)
            rel_ref[hh] = -slopes[hh] * rel

    qs = [q_ref[0, :, sl] for sl in hsl]
    gates = [_dot_nt_x3(kmean_ref[:, sl], q) for sl, q in zip(hsl, qs)]
    nidx = lax.broadcasted_iota(jnp.int32, (n_blk, blk), 0).astype(F32)
    neg_inf = jnp.full((n_blk, blk), -jnp.inf, F32)
    vals = [jnp.where(nidx < qi.astype(F32), gt, neg_inf) for gt in gates]
    sel_bias = [jnp.full((n_blk, blk), NEG_BIG, F32) for _ in range(hg)]
    for _ in range(MOBA_TOPK):
        mx = [jnp.max(v, axis=0, keepdims=True) for v in vals]
        is_max = [jnp.logical_and(v == m, m > -jnp.inf) for v, m in zip(vals, mx)]
        first = [jnp.min(jnp.where(im, nidx, float(n_blk)), axis=0, keepdims=True) for im in is_max]
        pick = [nidx == f for f in first]
        sel_bias = [jnp.where(pk, 0.0, sb) for pk, sb in zip(pick, sel_bias)]
        vals = [jnp.where(pk, neg_inf, v) for pk, v in zip(pick, vals)]
    sel_bias = [sb.T.astype(BF16) for sb in sel_bias]

    subs = [slice(i * MOBA_ROWS, (i + 1) * MOBA_ROWS) for i in range(blk // MOBA_ROWS)]
    chains = [(hh, sb) for hh in range(hg) for sb in subs]
    qbs = [qs[hh][sb, :].astype(BF16) for hh, sb in chains]
    sel_bs = [sel_bias[hh][sb, :] for hh, sb in chains]

    def attend(kbs, vbs, biases, carry):
        ss = [_bdot_nt(qb, kbs[hh]) * scale + bias for qb, (hh, _), bias in zip(qbs, chains, biases)]
        m_new = [jnp.max(x, axis=1, keepdims=True) for x in ss]
        if carry is not None:
            m_new = [jnp.maximum(m, mn) for (m, _, _), mn in zip(carry, m_new)]
        ps = [jnp.exp(x - m) for x, m in zip(ss, m_new)]
        ls = [jnp.sum(p, axis=1, keepdims=True) for p in ps]
        pv = [jnp.dot(p.astype(BF16), vbs[hh], preferred_element_type=F32) for p, (hh, _) in zip(ps, chains)]
        if carry is None:
            return tuple(zip(m_new, ls, pv))
        alphas = [jnp.exp(m - mn) for (m, _, _), mn in zip(carry, m_new)]
        return tuple((mn, al * l + ln, al * acc + x)
                     for (_, l, acc), mn, al, ln, x in zip(carry, m_new, alphas, ls, pv))

    own = pl.multiple_of(qi * blk, blk)
    col = lax.broadcasted_iota(jnp.int32, (MOBA_ROWS, blk), 1)
    row = lax.broadcasted_iota(jnp.int32, (MOBA_ROWS, blk), 0)
    own_bias = [jnp.where(col <= row + sb.start, rel_ref[hh, sb, 0:blk], NEG_BIG) for hh, sb in chains]
    carry0 = attend([kb_ref[pl.ds(own, blk), sl] for sl in hsl], [vb_ref[pl.ds(own, blk), sl] for sl in hsl],
                    own_bias, None)
    group_block = (lax.broadcasted_iota(jnp.int32, (n_blk, gw), 0)
                   - lax.broadcasted_iota(jnp.int32, (n_blk, gw), 1) // blk)

    def body(j, carry):
        n0 = MOBA_KGROUP * j
        start = pl.multiple_of(n0 * blk, gw)
        onehot = (group_block == n0).astype(BF16)
        dist = ((qi - n0) * blk).astype(F32)
        biases = [rel_ref[hh, sb, :] - slopes[hh] * dist + jnp.dot(sel, onehot, preferred_element_type=F32)
                  for (hh, sb), sel in zip(chains, sel_bs)]
        return attend([kb_ref[pl.ds(start, gw), sl] for sl in hsl], [vb_ref[pl.ds(start, gw), sl] for sl in hsl],
                      biases, carry)

    final = lax.fori_loop(0, (qi + MOBA_KGROUP - 1) // MOBA_KGROUP, body, carry0)
    for (hh, sb), (_, l, acc) in zip(chains, final):
        o_ref[0, sb, hsl[hh]] = acc / l


def moba_prompt(z3, slopes, *, heads):
    bsz, t, _ = z3.shape
    d = B_HEAD_DIM
    blk = MOBA_BLOCK
    hg = MOBA_HEADS_PER_STEP
    assert t % (MOBA_KGROUP * blk) == 0 and heads % hg == 0
    groups = heads // hg
    return pl.pallas_call(
        _moba_prompt_kernel,
        grid_spec=pltpu.PrefetchScalarGridSpec(
            num_scalar_prefetch=0,
            grid=(bsz, groups, t // blk),
            in_specs=[pl.BlockSpec(memory_space=pltpu.SMEM),
                      pl.BlockSpec((1, blk, hg * d), lambda b, g, i: (b, i, g)),
                      pl.BlockSpec((1, t, hg * d), lambda b, g, i: (b, 0, groups + g)),
                      pl.BlockSpec((1, t, hg * d), lambda b, g, i: (b, 0, 2 * groups + g))],
            out_specs=[pl.BlockSpec((1, blk, hg * d), lambda b, g, i: (b, i, g)),
                       pl.BlockSpec((1, hg, t, d), lambda b, g, i: (b, g, 0, 0)),
                       pl.BlockSpec((1, hg, t, d), lambda b, g, i: (b, g, 0, 0))],
            scratch_shapes=[pltpu.VMEM((t, hg * d), BF16), pltpu.VMEM((t, hg * d), BF16),
                            pltpu.VMEM((t // blk, hg * d), F32), pltpu.VMEM((hg, blk, MOBA_KGROUP * blk), F32)],
        ),
        out_shape=[jax.ShapeDtypeStruct((bsz, t, heads * d), F32),
                   jax.ShapeDtypeStruct((bsz, heads, t, d), F32),
                   jax.ShapeDtypeStruct((bsz, heads, t, d), F32)],
        compiler_params=_cparams(("parallel", "parallel", "arbitrary")),
        name="moba_prompt",
    )(slopes, z3, z3, z3)


def _paged_block_sum_kernel(pt_ref, *refs, pages_per_block):
    o_ref = refs[-1]
    for j in range(o_ref.shape[1]):
        acc = jnp.sum(refs[j * pages_per_block][...], axis=1)
        for s in range(1, pages_per_block):
            acc = acc + jnp.sum(refs[j * pages_per_block + s][...], axis=1)
        o_ref[0, j] = acc


def paged_block_sums(cache_k, page_table, *, pages_per_block, blocks_per_step):
    _, h, page, d = cache_k.shape
    bsz, n_pages = page_table.shape
    n_blk = n_pages // pages_per_block
    assert n_blk % blocks_per_step == 0
    pages_per_step = pages_per_block * blocks_per_step

    def page_spec(s):
        return pl.BlockSpec((None, h, page, d), lambda b, n, pt: (pt[b, n * pages_per_step + s], 0, 0, 0))

    return pl.pallas_call(
        functools.partial(_paged_block_sum_kernel, pages_per_block=pages_per_block),
        grid_spec=pltpu.PrefetchScalarGridSpec(
            num_scalar_prefetch=1,
            grid=(bsz, n_blk // blocks_per_step),
            in_specs=[page_spec(s) for s in range(pages_per_step)],
            out_specs=pl.BlockSpec((1, blocks_per_step, h, d), lambda b, n, pt: (b, n, 0, 0)),
        ),
        out_shape=jax.ShapeDtypeStruct((bsz, n_blk, h, d), F32),
        compiler_params=_cparams(("parallel", "arbitrary")),
        name="paged_block_sums",
    )(page_table, *([cache_k] * pages_per_step))


def _moba_select_kernel(ks_ref, q_ref, o_ref, *, block_len):
    n_blk, h = ks_ref.shape[1], ks_ref.shape[2]
    kmean = ks_ref[0] * (1.0 / block_len)
    gate = jnp.sum(kmean * q_ref[0], axis=-1, keepdims=True)
    nidx = lax.broadcasted_iota(jnp.int32, (n_blk, h, 1), 0).astype(F32)
    vals = gate
    for t in range(MOBA_TOPK):
        mx = jnp.max(vals, axis=0, keepdims=True)
        first = jnp.min(jnp.where(vals == mx, nidx, float(n_blk)), axis=0, keepdims=True)
        o_ref[0, t] = first[0].astype(jnp.int32)
        vals = jnp.where(nidx == first, -jnp.inf, vals)


def moba_select(ksum, q, *, block_len):
    bsz, n_blk, h, d = ksum.shape
    return pl.pallas_call(
        functools.partial(_moba_select_kernel, block_len=block_len),
        grid=(bsz,),
        in_specs=[pl.BlockSpec((1, n_blk, h, d), lambda b: (b, 0, 0, 0)),
                  pl.BlockSpec((1, h, d), lambda b: (b, 0, 0))],
        out_specs=pl.BlockSpec((1, MOBA_TOPK, h, 1), lambda b: (b, 0, 0, 0)),
        out_shape=jax.ShapeDtypeStruct((bsz, MOBA_TOPK, h, 1), jnp.int32),
        compiler_params=_cparams(("parallel",)),
        name="moba_select",
    )(ksum, q)


def _moba_decode_kernel(pt_ref, sel_ref, slope_ref, q_ref, kn_ref, vn_ref, ck_ref, cv_ref, o_ref,
                        kbuf, vbuf, sems, *, past_len, page, heads):
    b = pl.program_id(0)
    d = B_HEAD_DIM
    scale = d ** -0.5
    pages_per_block = MOBA_BLOCK // page
    n_slots = MOBA_TOPK * pages_per_block

    def page_copies(h, s):
        blk = sel_ref[b, s // pages_per_block, h]
        pg = pt_ref[b, blk * pages_per_block + s % pages_per_block]
        return (pltpu.make_async_copy(ck_ref.at[pg, h], kbuf.at[h, s], sems.at[0, h]),
                pltpu.make_async_copy(cv_ref.at[pg, h], vbuf.at[h, s], sems.at[1, h]))

    for h in range(heads):
        for s in range(n_slots):
            for cp in page_copies(h, s):
                cp.start()

    tok = lax.broadcasted_iota(jnp.int32, (page, 1), 0)
    for h in range(heads):
        for s in range(n_slots):
            for cp in page_copies(h, s):
                cp.wait()
        sl = slice(h * d, (h + 1) * d)
        slope = slope_ref[h]
        q = q_ref[0, :, sl]
        logits = []
        for s in range(n_slots):
            blk = sel_ref[b, s // pages_per_block, h]
            kpos = blk * MOBA_BLOCK + (s % pages_per_block) * page + tok
            sc = jnp.sum(kbuf[h, s] * q, axis=1, keepdims=True) * scale
            logits.append(sc - slope * (past_len - kpos).astype(F32))
        own = jnp.sum(q * kn_ref[0, :, sl], axis=1, keepdims=True) * scale
        m = own
        for sc in logits:
            m = jnp.maximum(m, jnp.max(sc, axis=0, keepdims=True))
        p_own = jnp.exp(own - m)
        l = p_own
        acc = p_own * vn_ref[0, :, sl]
        for s in range(n_slots):
            p = jnp.exp(logits[s] - m)
            l = l + jnp.sum(p, axis=0, keepdims=True)
            acc = acc + jnp.sum(p * vbuf[h, s], axis=0, keepdims=True)
        o_ref[0, :, sl] = acc / l


def moba_decode(z3, cache_k, cache_v, page_table, sel, slopes, *, past_len):
    bsz = z3.shape[0]
    _, heads, page, d = cache_k.shape
    w = heads * d
    n_slots = MOBA_TOPK * (MOBA_BLOCK // page)
    row_spec = lambda col: pl.BlockSpec((1, 1, w), lambda b, pt, sel_: (b, 0, col))
    return pl.pallas_call(
        functools.partial(_moba_decode_kernel, past_len=past_len, page=page, heads=heads),
        grid_spec=pltpu.PrefetchScalarGridSpec(
            num_scalar_prefetch=2,
            grid=(bsz,),
            in_specs=[pl.BlockSpec(memory_space=pltpu.SMEM), row_spec(0), row_spec(1), row_spec(2),
                      pl.BlockSpec(memory_space=pl.ANY), pl.BlockSpec(memory_space=pl.ANY)],
            out_specs=pl.BlockSpec((1, 1, w), lambda b, pt, sel_: (b, 0, 0)),
            scratch_shapes=[pltpu.VMEM((heads, n_slots, page, d), F32),
                            pltpu.VMEM((heads, n_slots, page, d), F32),
                            pltpu.SemaphoreType.DMA((2, heads))],
        ),
        out_shape=jax.ShapeDtypeStruct((bsz, 1, w), F32),
        compiler_params=_cparams(("arbitrary",)),
        name="moba_decode",
    )(page_table, sel, slopes, z3, z3, z3, cache_k, cache_v)


def _mem_attn_kernel(q_ref, k_ref, v_ref, o_ref, *, precise):
    tq = q_ref.shape[1]
    scale = C_HEAD_DIM ** -0.5
    rows = max(tq, 8)
    for h in range(C_HEADS):
        sl = slice(h * C_HEAD_DIM, (h + 1) * C_HEAD_DIM)
        q = jnp.broadcast_to(q_ref[0, :, sl], (rows, C_HEAD_DIM)) if tq < 8 else q_ref[0, :, sl]
        s = _mm_nt(q, k_ref[0, :, sl], precise) * scale
        m = jnp.max(s, axis=1, keepdims=True)
        p = jnp.exp(s - m)
        l = jnp.sum(p, axis=1, keepdims=True)
        o = _mm(p, v_ref[0, :, sl], precise) / l
        o_ref[0, :, sl] = o[0:tq, :]


def mem_attention(z3, mem_k3, mem_v3, *, q_col, k_col, v_col, tq, precise):
    bsz, t, _ = z3.shape
    mlen = mem_k3.shape[1]
    return pl.pallas_call(
        functools.partial(_mem_attn_kernel, precise=precise),
        grid=(bsz, t // tq),
        in_specs=[pl.BlockSpec((1, tq, C_WIDTH), lambda b, i: (b, i, q_col)),
                  pl.BlockSpec((1, mlen, C_WIDTH), lambda b, i: (b, 0, k_col)),
                  pl.BlockSpec((1, mlen, C_WIDTH), lambda b, i: (b, 0, v_col))],
        out_specs=pl.BlockSpec((1, tq, C_WIDTH), lambda b, i: (b, i, 0)),
        out_shape=jax.ShapeDtypeStruct((bsz, t, C_WIDTH), F32),
        compiler_params=_cparams(("parallel", "parallel")),
        name="mem_attention",
    )(z3, mem_k3, mem_v3)


def _out_proj_kernel(ymix_ref, ymem_ref, g0_ref, g1_ref, w_ref, g_ref, x_ref, o_ref, *, precise):
    half = g0_ref.shape[1]
    wmix = ymix_ref.shape[1]
    silu = lambda t: t * _sigmoid(t)
    g1 = g1_ref[...]
    u0 = ymix_ref[:, 0:half] * silu(g0_ref[...])
    u1 = ymix_ref[:, half:wmix] * silu(g1[:, 0:wmix - half])
    u2 = ymem_ref[...] * silu(g1[:, wmix - half:])
    y = _mm(u0, w_ref[0:half, :], precise)
    y = y + _mm(u1, w_ref[half:wmix, :], precise)
    y = y + _mm(u2, w_ref[wmix:, :], precise)
    o_ref[...] = x_ref[...] + _rms_scale(y, g_ref[...])


def out_proj(ymix, ymem, z, w_out, g, x, *, gate_col, tm):
    m, dm = x.shape
    wmix = ymix.shape[1]
    half = 1024
    return pl.pallas_call(
        functools.partial(_out_proj_kernel, precise=w_out.dtype == F32),
        grid=(m // tm,),
        in_specs=[pl.BlockSpec((tm, wmix), lambda i: (i, 0)),
                  pl.BlockSpec((tm, C_WIDTH), lambda i: (i, 0)),
                  pl.BlockSpec((tm, half), lambda i: (i, gate_col)),
                  pl.BlockSpec((tm, half), lambda i: (i, gate_col + 1)),
                  pl.BlockSpec(w_out.shape, lambda i: (0, 0)),
                  pl.BlockSpec((1, dm), lambda i: (0, 0)),
                  pl.BlockSpec((tm, dm), lambda i: (i, 0))],
        out_specs=pl.BlockSpec((tm, dm), lambda i: (i, 0)),
        out_shape=jax.ShapeDtypeStruct((m, dm), F32),
        compiler_params=_cparams(("parallel",)),
        name="out_proj",
    )(ymix, ymem, z, z, w_out, g, x)


def _prep_rwkv_weights(w_in, mu, decay_b, iclr_b, vres_b, a_width):
    rkv = 3 * a_width
    n_lora = mu.shape[0] - rkv
    pad = LORA_PAD - n_lora
    dm = w_in.shape[0]
    w = jnp.concatenate([w_in[:, :rkv], w_in[:, rkv + n_lora:], w_in[:, rkv:rkv + n_lora],
                         jnp.zeros((dm, pad), w_in.dtype)], axis=1)
    mu_r = mu[:rkv].reshape(1, rkv)
    mu_l = jnp.pad(mu[rkv:], (0, pad)).reshape(1, LORA_PAD)

    def pad_rows(m, start):
        return jnp.pad(m, ((start, LORA_PAD - start - m.shape[0]), (0, 0)))

    wd = pad_rows(decay_b, 0)
    wi = pad_rows(iclr_b, DECAY_LORA)
    wv = None if vres_b is None else pad_rows(vres_b, DECAY_LORA + ICLR_LORA)
    return w, mu_r, mu_l, wd, wi, wv


def kernel(x_prompt, x_sample, state_wkv, state_shift, cache_k_l1, cache_v_l1, cache_k_l3, cache_v_l3, cache_mem_k, cache_mem_v, page_table, mem_prompt, norm_pre, norm_post, w_in_l0, w_in_l1, w_in_l2, w_in_l3, w_mem_kv, w_out, mu_l0, mu_l2, rwkv_decay_b, rwkv_decay_0, rwkv_iclr_b, rwkv_iclr_0, rwkv_k_k, rwkv_k_a, rwkv_r_k, rwkv_gn_w, rwkv_gn_b, vres_b_l2, vres_0_l2):
    depth = norm_pre.shape[0]
    bp, seq, dm = x_prompt.shape
    bs, dec_seq, _ = x_sample.shape
    assert dec_seq == 1
    a_heads = state_wkv.shape[2]
    a_width = a_heads * A_HEAD_DIM
    n_pool, page, b_heads, _ = cache_k_l1.shape
    b_width = b_heads * B_HEAD_DIM
    mem_len = mem_prompt.shape[1]
    past_len = page_table.shape[1] * page
    assert past_len % MOBA_BLOCK == 0 and past_len // MOBA_BLOCK >= MOBA_TOPK and MOBA_BLOCK % page == 0
    assert a_width == b_width
    mix_w = a_width
    qmem_col = 3 * mix_w // C_WIDTH
    gate_col = (3 * mix_w + C_WIDTH) // 1024
    lora_col = (3 * mix_w + C_WIDTH + dm) // LORA_PAD

    zeros8 = jnp.zeros((a_width,), F32)
    rw = []
    for j, (w_in, mu, vb, v0) in enumerate(((w_in_l0, mu_l0, None, None), (w_in_l2, mu_l2, vres_b_l2, vres_0_l2))):
        w, mu_r, mu_l, wd, wi, wv = _prep_rwkv_weights(w_in, mu, rwkv_decay_b[j], rwkv_iclr_b[j], vb, a_width)
        pvec = jnp.stack([rwkv_decay_0[j], rwkv_iclr_0[j], zeros8 if v0 is None else v0, rwkv_k_k[j], rwkv_k_a[j],
                          zeros8, zeros8, zeros8])
        pvec2 = jnp.stack([rwkv_r_k[j].reshape(-1), rwkv_gn_w[j], rwkv_gn_b[j]] + [zeros8] * 5)
        lora = {True: (wd.astype(BF16), wi.astype(BF16), None if wv is None else wv.astype(BF16)),
                False: (wd, wi, wv)}
        rw.append(dict(w={True: w.astype(BF16), False: w}, lora=lora, mu_r=mu_r, mu_l=mu_l, pvec=pvec, pvec2=pvec2))
    w_moba = [{True: w.astype(BF16), False: w} for w in (w_in_l1, w_in_l3)]
    w_out_p = {True: w_out.astype(BF16), False: w_out}
    w_mem_all = jnp.transpose(w_mem_kv, (1, 0, 2)).reshape(dm, depth * 2 * C_WIDTH).astype(BF16)
    slopes = jnp.asarray(alibi_slopes(b_heads))
    ones_g = jnp.ones((1, dm), F32)
    caches = [tuple(jnp.transpose(c, (0, 2, 1, 3)) for c in pair)
              for pair in ((cache_k_l1, cache_v_l1), (cache_k_l3, cache_v_l3))]

    mem_rows = mem_prompt.reshape(bp * mem_len, dm)
    memkv = norm_matmul(mem_rows, ones_g, w_mem_all, normalize=False, tm=bp * mem_len, tn=1024)
    memkv3 = memkv.reshape(bp, mem_len, depth * 2 * C_WIDTH)
    memkv5 = memkv.reshape(bp, mem_len, depth, 2, C_HEADS, C_HEAD_DIM)
    p_mem_k = jnp.transpose(memkv5[:, :, :, 0], (2, 0, 1, 3, 4))
    p_mem_v = jnp.transpose(memkv5[:, :, :, 1], (2, 0, 1, 3, 4))
    cache_mem_k3 = cache_mem_k.reshape(depth, bs, mem_len, C_WIDTH)
    cache_mem_v3 = cache_mem_v.reshape(depth, bs, mem_len, C_WIDTH)

    def trunk(x3, is_prompt):
        bsz, t, _ = x3.shape
        m = bsz * t
        x = x3.reshape(m, dm)
        tm_a = 1024 if m % 1024 == 0 else m
        tm_f = 256 if m % 256 == 0 else m
        v_first = None
        wkv_out, shift_out, kv_out = [], [], []
        for i in range(depth):
            j = i // 2
            g_pre = norm_pre[i].reshape(1, dm)
            if i % 2 == 0:
                p = rw[j]
                z = norm_matmul(x, g_pre, p["w"][is_prompt], normalize=True, tm=tm_a, tn=768)
                z3 = z.reshape(bsz, t, -1)
                if is_prompt:
                    zprev3 = jnp.zeros((bsz, 1, z.shape[1]), F32)
                else:
                    zprev3 = norm_matmul(state_shift[j], ones_g, p["w"][is_prompt], normalize=False, tm=bsz, tn=768
                                         ).reshape(bsz, 1, -1)
                wd, wi, wv = p["lora"][is_prompt]
                r, lw, k, v, a, b = rwkv_prep(z3, zprev3, p["mu_r"], p["mu_l"], wd, wi, wv, p["pvec"],
                                              v_first, tt=min(t, 128), width=a_width, lora_col=lora_col)
                if v_first is None:
                    v_first = v
                if is_prompt:
                    y_mix, h_bd = rwkv_chunk_scan(r, lw, k, v, a, b, p["pvec2"], pairs_per_step=a_heads // 2)
                    hb = h_bd.reshape(bsz, a_heads // 2, 2, A_HEAD_DIM, 2, A_HEAD_DIM)
                    s_new = jnp.stack([hb[:, :, 0, :, 0, :], hb[:, :, 1, :, 1, :]], axis=2)
                    s_new = jnp.swapaxes(s_new.reshape(bsz, a_heads, A_HEAD_DIM, A_HEAD_DIM), -1, -2)
                else:
                    y_mix, s_new = rwkv_step(state_wkv[j], r, lw, k, v, a, b, rwkv_r_k[j], rwkv_gn_w[j], rwkv_gn_b[j])
                wkv_out.append(s_new)
                shift_out.append(rmsnorm_rows(x3[:, -1], g_pre))
            else:
                z = norm_matmul(x, g_pre, w_moba[j][is_prompt], normalize=True, tm=tm_a, tn=1024)
                z3 = z.reshape(bsz, t, -1)
                if is_prompt:
                    y_mix, k_hm, v_hm = moba_prompt(z3, slopes, heads=b_heads)
                    kv_out.append((jnp.transpose(k_hm, (0, 2, 1, 3)), jnp.transpose(v_hm, (0, 2, 1, 3))))
                else:
                    ck, cv = caches[j]
                    ksum = paged_block_sums(ck, page_table, pages_per_block=MOBA_BLOCK // page, blocks_per_step=4)
                    sel = moba_select(ksum, z3[:, 0, :b_width].reshape(bsz, b_heads, B_HEAD_DIM),
                                      block_len=MOBA_BLOCK)
                    y_mix = moba_decode(z3, ck, cv, page_table, sel[..., 0], slopes, past_len=past_len)
                    kv_out.append((z3[:, :, b_width:2 * b_width].reshape(bsz, t, b_heads, B_HEAD_DIM),
                                   z3[:, :, 2 * b_width:3 * b_width].reshape(bsz, t, b_heads, B_HEAD_DIM)))
            if is_prompt:
                y_mem = mem_attention(z3, memkv3, memkv3, q_col=qmem_col, k_col=2 * i, v_col=2 * i + 1,
                                      tq=min(t, 512), precise=False)
            else:
                y_mem = mem_attention(z3, cache_mem_k3[i], cache_mem_v3[i], q_col=qmem_col, k_col=0, v_col=0, tq=t,
                                      precise=True)
            x = out_proj(y_mix.reshape(m, mix_w), y_mem.reshape(m, C_WIDTH), z, w_out_p[is_prompt][i],
                         norm_post[i].reshape(1, dm), x, gate_col=gate_col, tm=tm_f)
            x3 = x.reshape(bsz, t, dm)
        return x3, jnp.stack(wkv_out), jnp.stack(shift_out), kv_out

    y_prompt, p_wkv, p_shift, p_kv = trunk(x_prompt, True)
    y_sample, s_wkv, s_shift, s_kv = trunk(x_sample, False)
    (p_k_l1, p_v_l1), (p_k_l3, p_v_l3) = p_kv
    (s_k_l1, s_v_l1), (s_k_l3, s_v_l3) = s_kv
    return (y_prompt, y_sample, p_wkv, p_shift, p_k_l1, p_v_l1, p_k_l3, p_v_l3, p_mem_k, p_mem_v,
            s_wkv, s_shift, s_k_l1, s_v_l1, s_k_l3, s_v_l3)
```

```python
import functools
import math

import numpy as np
import jax
import jax.numpy as jnp
from jax import lax
from jax.experimental import pallas as pl
from jax.experimental.pallas import tpu as pltpu

F32 = jnp.float32
BF16 = jnp.bfloat16

RMS_EPS = 1e-6
GN_EPS = 64e-5
A_HEAD_DIM = 64
B_HEAD_DIM = 128
C_HEADS = 4
C_HEAD_DIM = 128
C_WIDTH = C_HEADS * C_HEAD_DIM
MOBA_BLOCK = 256
MOBA_TOPK = 3
DECAY_LORA = 96
ICLR_LORA = 96
VRES_LORA = 64

LANES = 128
RWKV_CHUNK = 64
LORA_PAD = 512
NEG_BIG = -1e30
MOBA_ROWS = 128
MOBA_KGROUP = 2
MOBA_HEADS_PER_STEP = 2
VMEM_LIMIT = 48 * 1024 * 1024


def _cparams(sem):
    return pltpu.CompilerParams(dimension_semantics=sem, vmem_limit_bytes=VMEM_LIMIT)


def alibi_slopes(n):
    def pow2_slopes(m):
        start = 2.0 ** (-(2.0 ** -(math.log2(m) - 3)))
        return [start ** (i + 1) for i in range(m)]
    if math.log2(n).is_integer():
        s = pow2_slopes(n)
    else:
        c = 2 ** math.floor(math.log2(n))
        s = pow2_slopes(c) + pow2_slopes(2 * c)[0::2][: n - c]
    return np.asarray(s, dtype=np.float32)


def _bdot(a, b):
    return jnp.dot(a.astype(BF16), b.astype(BF16), preferred_element_type=F32)


def _bdot_nt(a, b):
    return lax.dot_general(a.astype(BF16), b.astype(BF16), (((1,), (1,)), ((), ())),
                           preferred_element_type=F32)


def _mm(a, b, precise):
    if precise:
        return jnp.dot(a, b, preferred_element_type=F32, precision=lax.Precision.HIGHEST)
    return _bdot(a, b)


def _mm_nt(a, b, precise):
    if precise:
        return lax.dot_general(a, b, (((1,), (1,)), ((), ())), preferred_element_type=F32,
                               precision=lax.Precision.HIGHEST)
    return _bdot_nt(a, b)


def _split3(x):
    hi = x.astype(BF16)
    r1 = x - hi.astype(F32)
    mid = r1.astype(BF16)
    lo = (r1 - mid.astype(F32)).astype(BF16)
    return hi, mid, lo


def _dot_rhs01(x, q01):
    hi, mid, lo = _split3(x)
    acc = jnp.dot(lo, q01, preferred_element_type=F32)
    acc = acc + jnp.dot(mid, q01, preferred_element_type=F32)
    return acc + jnp.dot(hi, q01, preferred_element_type=F32)


def _dot_lhs01(t01, x):
    hi, mid, lo = _split3(x)
    acc = jnp.dot(t01, lo, preferred_element_type=F32)
    acc = acc + jnp.dot(t01, mid, preferred_element_type=F32)
    return acc + jnp.dot(t01, hi, preferred_element_type=F32)


def _dot_nt_x3(a, b):
    ah = a.astype(BF16)
    al = (a - ah.astype(F32)).astype(BF16)
    bh = b.astype(BF16)
    bl = (b - bh.astype(F32)).astype(BF16)
    dn = (((1,), (1,)), ((), ()))
    acc = lax.dot_general(al, bh, dn, preferred_element_type=F32)
    acc = acc + lax.dot_general(ah, bl, dn, preferred_element_type=F32)
    return acc + lax.dot_general(ah, bh, dn, preferred_element_type=F32)


def _head_sum_matrix(width, head):
    r = lax.broadcasted_iota(jnp.int32, (width, width), 0) // head
    c = lax.broadcasted_iota(jnp.int32, (width, width), 1) // head
    return (r == c).astype(BF16)


def _sigmoid(x):
    return 1.0 / (1.0 + jnp.exp(-x))


def _softplus(x):
    return jnp.maximum(x, 0.0) + jnp.log(1.0 + jnp.exp(-jnp.abs(x)))


def _rms_scale(x, g):
    ms = jnp.mean(x * x, axis=-1, keepdims=True)
    return x * lax.rsqrt(ms + RMS_EPS) * g


def _norm_matmul_kernel(x_ref, g_ref, w_ref, o_ref, h_ref, *, normalize, row_block, precise):
    @pl.when(pl.program_id(1) == 0)
    def _():
        tm = x_ref.shape[0]
        for s in range(0, tm, row_block):
            x = x_ref[s:s + row_block, :]
            if normalize:
                x = _rms_scale(x, g_ref[...])
            h_ref[s:s + row_block, :] = x.astype(h_ref.dtype)
    o_ref[...] = _mm(h_ref[...], w_ref[...], precise)


def norm_matmul(x, g, w, *, normalize, tm, tn):
    m, k = x.shape
    n = w.shape[1]
    assert m % tm == 0 and n % tn == 0, (m, tm, n, tn)
    row_block = min(tm, 256)
    precise = w.dtype == F32
    return pl.pallas_call(
        functools.partial(_norm_matmul_kernel, normalize=normalize, row_block=row_block, precise=precise),
        grid=(m // tm, n // tn),
        in_specs=[pl.BlockSpec((tm, k), lambda i, j: (i, 0)),
                  pl.BlockSpec((1, k), lambda i, j: (0, 0)),
                  pl.BlockSpec((k, tn), lambda i, j: (0, j))],
        out_specs=pl.BlockSpec((tm, tn), lambda i, j: (i, j)),
        out_shape=jax.ShapeDtypeStruct((m, n), F32),
        scratch_shapes=[pltpu.VMEM((tm, k), w.dtype)],
        compiler_params=_cparams(("parallel", "arbitrary")),
        name="norm_matmul" if normalize else "plain_matmul",
    )(x, g, w)


def _rmsnorm_rows_kernel(x_ref, g_ref, o_ref):
    o_ref[...] = _rms_scale(x_ref[...], g_ref[...])


def rmsnorm_rows(x, g):
    return pl.pallas_call(
        _rmsnorm_rows_kernel,
        out_shape=jax.ShapeDtypeStruct(x.shape, F32),
        name="rmsnorm_rows",
    )(x, g)


def _rwkv_prep_kernel(*refs, has_vres, width, precise):
    if has_vres:
        (zr_ref, zl_ref, pfr_ref, pfl_ref, mur_ref, mul_ref, wd_ref, wi_ref, wv_ref, pv_ref, vf_ref,
         r_o, lw_o, k_o, v_o, a_o, b_o, carry_r, carry_l) = refs
    else:
        (zr_ref, zl_ref, pfr_ref, pfl_ref, mur_ref, mul_ref, wd_ref, wi_ref, pv_ref,
         r_o, lw_o, k_o, v_o, a_o, b_o, carry_r, carry_l) = refs
        wv_ref = vf_ref = None
    tt = zr_ref.shape[1]

    @pl.when(pl.program_id(1) == 0)
    def _():
        carry_r[...] = pfr_ref[0]
        carry_l[...] = pfl_ref[0]

    if tt > 1:
        first_row = lax.broadcasted_iota(jnp.int32, (tt, 1), 0) == 0

    def shift_mix(z, carry, mu):
        if tt > 1:
            prev = jnp.where(first_row, carry, pltpu.roll(z, 1, 0))
        else:
            prev = carry
        return z + mu * (prev - z)

    zl = zl_ref[0]
    ls = shift_mix(zl, carry_l[...], mul_ref[...])
    carry_l[...] = zl[tt - 1:tt, :]

    decay_0 = pv_ref[0:1, :]
    iclr_0 = pv_ref[1:2, :]
    vres_0 = pv_ref[2:3, :]
    k_k = pv_ref[3:4, :]
    k_a = pv_ref[4:5, :]

    log_w = -_softplus(-(decay_0 + _mm(jnp.tanh(ls), wd_ref[...], precise))) - 0.5
    lw_o[0] = -jnp.exp(log_w)
    rate = _sigmoid(iclr_0 + _mm(ls, wi_ref[...], precise))

    def seg(i):
        z = zr_ref[0, :, i * width:(i + 1) * width]
        out = shift_mix(z, carry_r[:, i * width:(i + 1) * width], mur_ref[:, i * width:(i + 1) * width])
        carry_r[:, i * width:(i + 1) * width] = z[tt - 1:tt, :]
        return out

    r_o[0] = seg(0)
    k = seg(1)
    v = seg(2)
    if has_vres:
        v = v + (vf_ref[0] - v) * _sigmoid(vres_0 + _mm(ls, wv_ref[...], precise))
    v_o[0] = v

    q01 = _head_sum_matrix(LANES, A_HEAD_DIM)
    kk = k * k_k
    for p in range(width // LANES):
        sl = slice(p * LANES, (p + 1) * LANES)
        kp = kk[:, sl]
        norm = jnp.sqrt(_dot_rhs01(kp * kp, q01))
        kp = kp / jnp.maximum(norm, 1e-12)
        a_o[0, :, sl] = -kp
        b_o[0, :, sl] = kp * rate[:, sl]
    k_o[0] = k * (1.0 + (rate - 1.0) * k_a)


def rwkv_prep(z3, zprev3, mu_r, mu_l, wd, wi, wv, pvec, v_first, *, tt, width, lora_col):
    b, t, _ = z3.shape
    has_vres = wv is not None
    rkv = 3 * width
    in_specs = [
        pl.BlockSpec((1, tt, rkv), lambda i, j: (i, j, 0)),
        pl.BlockSpec((1, tt, LORA_PAD), lambda i, j: (i, j, lora_col)),
        pl.BlockSpec((1, 1, rkv), lambda i, j: (i, 0, 0)),
        pl.BlockSpec((1, 1, LORA_PAD), lambda i, j: (i, 0, lora_col)),
        pl.BlockSpec((1, rkv), lambda i, j: (0, 0)),
        pl.BlockSpec((1, LORA_PAD), lambda i, j: (0, 0)),
        pl.BlockSpec((LORA_PAD, width), lambda i, j: (0, 0)),
        pl.BlockSpec((LORA_PAD, width), lambda i, j: (0, 0)),
    ]
    args = [z3, z3, zprev3, zprev3, mu_r, mu_l, wd, wi]
    if has_vres:
        in_specs.append(pl.BlockSpec((LORA_PAD, width), lambda i, j: (0, 0)))
        args.append(wv)
    in_specs.append(pl.BlockSpec((8, width), lambda i, j: (0, 0)))
    args.append(pvec)
    if has_vres:
        in_specs.append(pl.BlockSpec((1, tt, width), lambda i, j: (i, j, 0)))
        args.append(v_first)
    out_spec = pl.BlockSpec((1, tt, width), lambda i, j: (i, j, 0))
    out_sds = jax.ShapeDtypeStruct((b, t, width), F32)
    return pl.pallas_call(
        functools.partial(_rwkv_prep_kernel, has_vres=has_vres, width=width, precise=wd.dtype == F32),
        grid=(b, t // tt),
        in_specs=in_specs,
        out_specs=[out_spec] * 6,
        out_shape=[out_sds] * 6,
        scratch_shapes=[pltpu.VMEM((1, rkv), F32), pltpu.VMEM((1, LORA_PAD), F32)],
        compiler_params=_cparams(("parallel", "arbitrary")),
        name="rwkv_prep",
    )(*args)


def _stack_heads(x, lane_is_head0):
    zero = jnp.zeros_like(x)
    return jnp.concatenate([jnp.where(lane_is_head0, x, zero), jnp.where(lane_is_head0, zero, x)], axis=0)


def _rwkv_group_chunk(rs, lws, ks, vs, as_, bs, hs, c):
    L = rs[0].shape[0]
    n2 = 2 * L
    each = lambda f, *xs: [f(*t) for t in zip(*xs)]
    st = functools.partial(_stack_heads, lane_is_head0=c["head0"])
    zero = jnp.zeros((n2, n2), F32)
    cum = [_dot_lhs01(c["tri"], lw) for lw in lws]
    tot = [x[L - 1:L, :] for x in cum]
    a_st = each(lambda a, cm, lw: st(a * jnp.exp(cm - lw)), as_, cum, lws)
    r_st = each(lambda r, cm: st(r * jnp.exp(cm)), rs, cum)
    v_st = [st(v) for v in vs]
    g_inv = [jnp.exp(-cm) for cm in cum]
    rhs = each(lambda b, k, g: jnp.concatenate([st(b * g), st(k * g)], axis=0), bs, ks, g_inv)
    gram = each(lambda a, r, x: _bdot_nt(jnp.concatenate([a, r], axis=0), x), a_st, r_st, rhs)
    a_ab = [jnp.where(c["strict"], g[0:n2, 0:n2], zero) for g in gram]
    a_ak = [jnp.where(c["strict"], g[0:n2, n2:2 * n2], zero) for g in gram]
    a_rb = [jnp.where(c["incl"], g[n2:2 * n2, 0:n2], zero) for g in gram]
    a_rk = [jnp.where(c["incl"], g[n2:2 * n2, n2:2 * n2], zero) for g in gram]
    pw = a_ab
    inv = [c["eye"] + x for x in a_ab]
    n = 1
    while 2 * n < L:
        pw = [_bdot(x, x) for x in pw]
        inv = each(lambda i, p: i + _bdot(i, p), inv, pw)
        n *= 2
    a_p = each(_bdot, inv, a_st)
    av = each(_bdot, a_ak, v_st)
    u0 = each(_bdot, inv, av)
    uv = each(lambda u, v: jnp.concatenate([u, v], axis=0), u0, v_st)
    g_end = each(lambda t, cm: jnp.exp(t - cm), tot, cum)
    bk_end_t = each(lambda b, k, g: jnp.concatenate([st(b * g), st(k * g)], axis=0).T, bs, ks, g_end)
    m_bd = each(lambda t, bt, ap: c["eye"] * jnp.exp(t) + _bdot(bt[:, 0:n2], ap), tot, bk_end_t, a_p)
    c_bd = each(_bdot, bk_end_t, uv)
    r_p = each(lambda r, arb, ap: r + _bdot(arb, ap), r_st, a_rb, a_p)
    y0 = each(lambda arb, ark, x: _bdot(jnp.concatenate([arb, ark], axis=1), x), a_rb, a_rk, uv)
    y_st = each(lambda rp, h, y: _bdot(rp, h) + y, r_p, hs, y0)
    h_new = each(lambda m, h, cc: _bdot(m, h) + cc, m_bd, hs, c_bd)
    return [y[0:L, :] + y[L:n2, :] for y in y_st], h_new


def _rwkv_chunk_kernel(pt_ref, r_ref, lw_ref, k_ref, v_ref, a_ref, b_ref, pv_ref, *refs, pairs, n_pages, pages_per_block):
    page_refs = refs[:n_pages]
    y_ref, hout_ref, ksum_ref, h_ref = refs[n_pages:]
    L = r_ref.shape[1]
    n2 = 2 * L
    for j in range(n_pages // pages_per_block):
        acc = jnp.sum(page_refs[j * pages_per_block][...], axis=1)
        for s in range(1, pages_per_block):
            acc = acc + jnp.sum(page_refs[j * pages_per_block + s][...], axis=1)
        ksum_ref[0, j] = acc

    @pl.when(pl.program_id(1) == 0)
    def _():
        h_ref[...] = jnp.zeros_like(h_ref)

    ri = lax.broadcasted_iota(jnp.int32, (n2, n2), 0)
    ci = lax.broadcasted_iota(jnp.int32, (n2, n2), 1)
    consts = {
        "tri": (lax.broadcasted_iota(jnp.int32, (L, L), 0) >= lax.broadcasted_iota(jnp.int32, (L, L), 1)).astype(BF16),
        "head0": lax.broadcasted_iota(jnp.int32, (1, LANES), 1) < A_HEAD_DIM,
        "strict": (ri % L) > (ci % L),
        "incl": (ri % L) >= (ci % L),
        "eye": (ri == ci).astype(F32),
    }
    sls = [slice(p * LANES, (p + 1) * LANES) for p in range(pairs)]
    load = lambda ref: [ref[0, :, sl] for sl in sls]
    rs, ks, vs = load(r_ref), load(k_ref), load(v_ref)
    ys, h_new = _rwkv_group_chunk(rs, load(lw_ref), ks, vs, load(a_ref), load(b_ref),
                                  [h_ref[p] for p in range(pairs)], consts)
    for p in range(pairs):
        h_ref[p] = h_new[p]
    q01 = _head_sum_matrix(LANES, A_HEAD_DIM)
    inv_n = 1.0 / A_HEAD_DIM
    mean = [_dot_rhs01(y, q01) * inv_n for y in ys]
    dev = [y - m for y, m in zip(ys, mean)]
    var = [_dot_rhs01(d * d, q01) * inv_n for d in dev]
    bonus = [_dot_rhs01(r * k * pv_ref[0:1, sl], q01) * v for r, k, v, sl in zip(rs, ks, vs, sls)]
    for p, sl in enumerate(sls):
        y_ref[0, :, sl] = dev[p] * lax.rsqrt(var[p] + GN_EPS) * pv_ref[1:2, sl] + pv_ref[2:3, sl] + bonus[p]

    @pl.when(pl.program_id(1) == pl.num_programs(1) - 1)
    def _():
        hout_ref[0] = h_ref[...]


def rwkv_chunk_scan(r, lw, k, v, a, b, pvec, cache_k, page_table, *, pages_per_block):
    bsz, t, w = r.shape
    n_pairs = w // LANES
    n_chunks = t // RWKV_CHUNK
    assert t % RWKV_CHUNK == 0
    _, ch, page, cd = cache_k.shape
    bs, seq_pages = page_table.shape
    steps = bsz * n_chunks
    n_pages = bs * seq_pages // steps
    assert n_pages * steps == bs * seq_pages and seq_pages % n_pages == 0 and n_pages % pages_per_block == 0
    steps_per_seq = seq_pages // n_pages
    blocks_per_step = n_pages // pages_per_block
    io_spec = pl.BlockSpec((1, RWKV_CHUNK, w), lambda i, c, pt: (i, c, 0))

    def page_spec(s):
        def idx(i, c, pt):
            step = i * n_chunks + c
            return (pt[step // steps_per_seq, (step % steps_per_seq) * n_pages + s], 0, 0, 0)
        return pl.BlockSpec((None, ch, page, cd), idx)

    def ksum_idx(i, c, pt):
        step = i * n_chunks + c
        return (step // steps_per_seq, step % steps_per_seq, 0, 0)

    return pl.pallas_call(
        functools.partial(_rwkv_chunk_kernel, pairs=n_pairs, n_pages=n_pages, pages_per_block=pages_per_block),
        grid_spec=pltpu.PrefetchScalarGridSpec(
            num_scalar_prefetch=1,
            grid=(bsz, n_chunks),
            in_specs=[io_spec] * 6 + [pl.BlockSpec((8, w), lambda i, c, pt: (0, 0))]
                     + [page_spec(s) for s in range(n_pages)],
            out_specs=[io_spec, pl.BlockSpec((1, n_pairs, LANES, LANES), lambda i, c, pt: (i, 0, 0, 0)),
                       pl.BlockSpec((1, blocks_per_step, ch, cd), ksum_idx)],
            scratch_shapes=[pltpu.VMEM((n_pairs, LANES, LANES), F32)],
        ),
        out_shape=[jax.ShapeDtypeStruct((bsz, t, w), F32),
                   jax.ShapeDtypeStruct((bsz, n_pairs, LANES, LANES), F32),
                   jax.ShapeDtypeStruct((bs, seq_pages // pages_per_block, ch, cd), F32)],
        compiler_params=_cparams(("parallel", "arbitrary")),
        name="rwkv_chunk_scan",
    )(page_table, r, lw, k, v, a, b, pvec, *([cache_k] * n_pages))


def _rwkv_step_kernel(s_ref, r_ref, lw_ref, k_ref, a_ref, b_ref, v_ref, rk_ref, gw_ref, gb_ref, y_ref, so_ref):
    s = s_ref[0]
    r = r_ref[0]
    k = k_ref[0]
    v = v_ref[0]
    sa = jnp.sum(s * a_ref[0], axis=-1, keepdims=True)
    s_new = s * jnp.exp(lw_ref[0]) + sa * b_ref[0] + v * k
    so_ref[0] = s_new
    y = jnp.sum(s_new * r, axis=-1, keepdims=True)
    mean = jnp.mean(y, axis=1, keepdims=True)
    d = y - mean
    var = jnp.mean(d * d, axis=1, keepdims=True)
    bonus = jnp.sum(r * k * rk_ref[...], axis=-1, keepdims=True) * v
    y_ref[0] = d * lax.rsqrt(var + GN_EPS) * gw_ref[...] + gb_ref[...] + bonus


def rwkv_step(state, r, lw, k, v, a, b, r_k, gn_w, gn_b):
    bsz, h, n, _ = state.shape
    row = lambda x: x.reshape(bsz, h, 1, n)
    col = lambda x: x.reshape(bsz, h, n, 1)
    row_spec = pl.BlockSpec((1, h, 1, n), lambda i: (i, 0, 0, 0))
    col_spec = pl.BlockSpec((1, h, n, 1), lambda i: (i, 0, 0, 0))
    st_spec = pl.BlockSpec((1, h, n, n), lambda i: (i, 0, 0, 0))
    prow = pl.BlockSpec((h, 1, n), lambda i: (0, 0, 0))
    pcol = pl.BlockSpec((h, n, 1), lambda i: (0, 0, 0))
    y, s_new = pl.pallas_call(
        _rwkv_step_kernel,
        grid=(bsz,),
        in_specs=[st_spec, row_spec, row_spec, row_spec, row_spec, row_spec, col_spec, prow, pcol, pcol],
        out_specs=[col_spec, st_spec],
        out_shape=[jax.ShapeDtypeStruct((bsz, h, n, 1), F32), jax.ShapeDtypeStruct((bsz, h, n, n), F32)],
        compiler_params=_cparams(("parallel",)),
        name="rwkv_step",
    )(state, row(r), row(lw), row(k), row(a), row(b), col(v),
      r_k.reshape(h, 1, n), gn_w.reshape(h, n, 1), gn_b.reshape(h, n, 1))
    return y.reshape(bsz, 1, h * n), s_new


def _moba_prompt_kernel(slope_ref, q_ref, k_ref, v_ref, o_ref, ko_ref, vo_ref, kb_ref, vb_ref, kmean_ref, rel_ref):
    g = pl.program_id(1)
    qi = pl.program_id(2)
    blk = q_ref.shape[1]
    t = k_ref.shape[1]
    n_blk = t // blk
    d = B_HEAD_DIM
    hg = q_ref.shape[2] // d
    scale = d ** -0.5
    hsl = [slice(hh * d, (hh + 1) * d) for hh in range(hg)]
    slopes = [slope_ref[g * hg + hh] for hh in range(hg)]
    gw = MOBA_KGROUP * blk

    @pl.when(qi == 0)
    def _():
        rel = (lax.broadcasted_iota(jnp.int32, (blk, gw), 0)
               - lax.broadcasted_iota(jnp.int32, (blk, gw), 1)).astype(F32)
        for hh in range(hg):
            kf = k_ref[0, :, hsl[hh]]
            vf = v_ref[0, :, hsl[hh]]
            ko_ref[0, hh] = kf
            vo_ref[0, hh] = vf
            kb_ref[:, hsl[hh]] = kf.astype(BF16)
            vb_ref[:, hsl[hh]] = vf.astype(BF16)
            kmean_ref[:, hsl[hh]] = jnp.sum(kf.reshape(n_blk, blk, d), axis=1) * (1.0 / blk)
            rel_ref[hh] = -slopes[hh] * rel

    qs = [q_ref[0, :, sl] for sl in hsl]
    gates = [_dot_nt_x3(kmean_ref[:, sl], q) for sl, q in zip(hsl, qs)]
    nidx = lax.broadcasted_iota(jnp.int32, (n_blk, blk), 0).astype(F32)
    neg_inf = jnp.full((n_blk, blk), -jnp.inf, F32)
    vals = [jnp.where(nidx < qi.astype(F32), gt, neg_inf) for gt in gates]
    sel_bias = [jnp.full((n_blk, blk), NEG_BIG, F32) for _ in range(hg)]
    for _ in range(MOBA_TOPK):
        mx = [jnp.max(v, axis=0, keepdims=True) for v in vals]
        is_max = [jnp.logical_and(v == m, m > -jnp.inf) for v, m in zip(vals, mx)]
        first = [jnp.min(jnp.where(im, nidx, float(n_blk)), axis=0, keepdims=True) for im in is_max]
        pick = [nidx == f for f in first]
        sel_bias = [jnp.where(pk, 0.0, sb) for pk, sb in zip(pick, sel_bias)]
        vals = [jnp.where(pk, neg_inf, v) for pk, v in zip(pick, vals)]
    sel_bias = [sb.T.astype(BF16) for sb in sel_bias]

    subs = [slice(i * MOBA_ROWS, (i + 1) * MOBA_ROWS) for i in range(blk // MOBA_ROWS)]
    chains = [(hh, sb) for hh in range(hg) for sb in subs]
    qbs = [qs[hh][sb, :].astype(BF16) for hh, sb in chains]
    sel_bs = [sel_bias[hh][sb, :] for hh, sb in chains]

    def attend(kbs, vbs, biases, carry):
        ss = [_bdot_nt(qb, kbs[hh]) * scale + bias for qb, (hh, _), bias in zip(qbs, chains, biases)]
        m_new = [jnp.max(x, axis=1, keepdims=True) for x in ss]
        if carry is not None:
            m_new = [jnp.maximum(m, mn) for (m, _, _), mn in zip(carry, m_new)]
        ps = [jnp.exp(x - m) for x, m in zip(ss, m_new)]
        ls = [jnp.sum(p, axis=1, keepdims=True) for p in ps]
        pv = [jnp.dot(p.astype(BF16), vbs[hh], preferred_element_type=F32) for p, (hh, _) in zip(ps, chains)]
        if carry is None:
            return tuple(zip(m_new, ls, pv))
        alphas = [jnp.exp(m - mn) for (m, _, _), mn in zip(carry, m_new)]
        return tuple((mn, al * l + ln, al * acc + x)
                     for (_, l, acc), mn, al, ln, x in zip(carry, m_new, alphas, ls, pv))

    own = pl.multiple_of(qi * blk, blk)
    col = lax.broadcasted_iota(jnp.int32, (MOBA_ROWS, blk), 1)
    row = lax.broadcasted_iota(jnp.int32, (MOBA_ROWS, blk), 0)
    own_bias = [jnp.where(col <= row + sb.start, rel_ref[hh, sb, 0:blk], NEG_BIG) for hh, sb in chains]
    carry0 = attend([kb_ref[pl.ds(own, blk), sl] for sl in hsl], [vb_ref[pl.ds(own, blk), sl] for sl in hsl],
                    own_bias, None)
    group_block = (lax.broadcasted_iota(jnp.int32, (n_blk, gw), 0)
                   - lax.broadcasted_iota(jnp.int32, (n_blk, gw), 1) // blk)

    def body(j, carry):
        n0 = MOBA_KGROUP * j
        start = pl.multiple_of(n0 * blk, gw)
        onehot = (group_block == n0).astype(BF16)
        dist = ((qi - n0) * blk).astype(F32)
        biases = [rel_ref[hh, sb, :] - slopes[hh] * dist + jnp.dot(sel, onehot, preferred_element_type=F32)
                  for (hh, sb), sel in zip(chains, sel_bs)]
        return attend([kb_ref[pl.ds(start, gw), sl] for sl in hsl], [vb_ref[pl.ds(start, gw), sl] for sl in hsl],
                      biases, carry)

    final = lax.fori_loop(0, (qi + MOBA_KGROUP - 1) // MOBA_KGROUP, body, carry0)
    for (hh, sb), (_, l, acc) in zip(chains, final):
        o_ref[0, sb, hsl[hh]] = acc / l


def moba_prompt(z3, slopes, *, heads):
    bsz, t, _ = z3.shape
    d = B_HEAD_DIM
    blk = MOBA_BLOCK
    hg = MOBA_HEADS_PER_STEP
    assert t % (MOBA_KGROUP * blk) == 0 and heads % hg == 0
    groups = heads // hg
    return pl.pallas_call(
        _moba_prompt_kernel,
        grid_spec=pltpu.PrefetchScalarGridSpec(
            num_scalar_prefetch=0,
            grid=(bsz, groups, t // blk),
            in_specs=[pl.BlockSpec(memory_space=pltpu.SMEM),
                      pl.BlockSpec((1, blk, hg * d), lambda b, g, i: (b, i, g)),
                      pl.BlockSpec((1, t, hg * d), lambda b, g, i: (b, 0, groups + g)),
                      pl.BlockSpec((1, t, hg * d), lambda b, g, i: (b, 0, 2 * groups + g))],
            out_specs=[pl.BlockSpec((1, blk, hg * d), lambda b, g, i: (b, i, g)),
                       pl.BlockSpec((1, hg, t, d), lambda b, g, i: (b, g, 0, 0)),
                       pl.BlockSpec((1, hg, t, d), lambda b, g, i: (b, g, 0, 0))],
            scratch_shapes=[pltpu.VMEM((t, hg * d), BF16), pltpu.VMEM((t, hg * d), BF16),
                            pltpu.VMEM((t // blk, hg * d), F32), pltpu.VMEM((hg, blk, MOBA_KGROUP * blk), F32)],
        ),
        out_shape=[jax.ShapeDtypeStruct((bsz, t, heads * d), F32),
                   jax.ShapeDtypeStruct((bsz, heads, t, d), F32),
                   jax.ShapeDtypeStruct((bsz, heads, t, d), F32)],
        compiler_params=_cparams(("parallel", "parallel", "arbitrary")),
        name="moba_prompt",
    )(slopes, z3, z3, z3)


def _moba_select_kernel(ks_ref, q_ref, o_ref, *, block_len):
    n_blk, h = ks_ref.shape[1], ks_ref.shape[2]
    kmean = ks_ref[0] * (1.0 / block_len)
    gate = jnp.sum(kmean * q_ref[0], axis=-1, keepdims=True)
    nidx = lax.broadcasted_iota(jnp.int32, (n_blk, h, 1), 0).astype(F32)
    vals = gate
    for t in range(MOBA_TOPK):
        mx = jnp.max(vals, axis=0, keepdims=True)
        first = jnp.min(jnp.where(vals == mx, nidx, float(n_blk)), axis=0, keepdims=True)
        o_ref[0, t] = first[0].astype(jnp.int32)
        vals = jnp.where(nidx == first, -jnp.inf, vals)


def moba_select(ksum, q, *, block_len):
    bsz, n_blk, h, d = ksum.shape
    return pl.pallas_call(
        functools.partial(_moba_select_kernel, block_len=block_len),
        grid=(bsz,),
        in_specs=[pl.BlockSpec((1, n_blk, h, d), lambda b: (b, 0, 0, 0)),
                  pl.BlockSpec((1, h, d), lambda b: (b, 0, 0))],
        out_specs=pl.BlockSpec((1, MOBA_TOPK, h, 1), lambda b: (b, 0, 0, 0)),
        out_shape=jax.ShapeDtypeStruct((bsz, MOBA_TOPK, h, 1), jnp.int32),
        compiler_params=_cparams(("parallel",)),
        name="moba_select",
    )(ksum, q)


def _moba_decode_kernel(pt_ref, sel_ref, slope_ref, q_ref, kn_ref, vn_ref, ck_ref, cv_ref, o_ref,
                        kbuf, vbuf, sems, *, past_len, page, heads):
    b = pl.program_id(0)
    d = B_HEAD_DIM
    scale = d ** -0.5
    pages_per_block = MOBA_BLOCK // page
    n_slots = MOBA_TOPK * pages_per_block

    def page_copies(h, s):
        blk = sel_ref[b, s // pages_per_block, h]
        pg = pt_ref[b, blk * pages_per_block + s % pages_per_block]
        return (pltpu.make_async_copy(ck_ref.at[pg, h], kbuf.at[h, s], sems.at[0, h]),
                pltpu.make_async_copy(cv_ref.at[pg, h], vbuf.at[h, s], sems.at[1, h]))

    for h in range(heads):
        for s in range(n_slots):
            for cp in page_copies(h, s):
                cp.start()

    tok = lax.broadcasted_iota(jnp.int32, (page, 1), 0)
    for h in range(heads):
        for s in range(n_slots):
            for cp in page_copies(h, s):
                cp.wait()
        sl = slice(h * d, (h + 1) * d)
        slope = slope_ref[h]
        q = q_ref[0, :, sl]
        logits = []
        for s in range(n_slots):
            blk = sel_ref[b, s // pages_per_block, h]
            kpos = blk * MOBA_BLOCK + (s % pages_per_block) * page + tok
            sc = jnp.sum(kbuf[h, s] * q, axis=1, keepdims=True) * scale
            logits.append(sc - slope * (past_len - kpos).astype(F32))
        own = jnp.sum(q * kn_ref[0, :, sl], axis=1, keepdims=True) * scale
        m = own
        for sc in logits:
            m = jnp.maximum(m, jnp.max(sc, axis=0, keepdims=True))
        p_own = jnp.exp(own - m)
        l = p_own
        acc = p_own * vn_ref[0, :, sl]
        for s in range(n_slots):
            p = jnp.exp(logits[s] - m)
            l = l + jnp.sum(p, axis=0, keepdims=True)
            acc = acc + jnp.sum(p * vbuf[h, s], axis=0, keepdims=True)
        o_ref[0, :, sl] = acc / l


def moba_decode(z3, cache_k, cache_v, page_table, sel, slopes, *, past_len):
    bsz = z3.shape[0]
    _, heads, page, d = cache_k.shape
    w = heads * d
    n_slots = MOBA_TOPK * (MOBA_BLOCK // page)
    row_spec = lambda col: pl.BlockSpec((1, 1, w), lambda b, pt, sel_: (b, 0, col))
    return pl.pallas_call(
        functools.partial(_moba_decode_kernel, past_len=past_len, page=page, heads=heads),
        grid_spec=pltpu.PrefetchScalarGridSpec(
            num_scalar_prefetch=2,
            grid=(bsz,),
            in_specs=[pl.BlockSpec(memory_space=pltpu.SMEM), row_spec(0), row_spec(1), row_spec(2),
                      pl.BlockSpec(memory_space=pl.ANY), pl.BlockSpec(memory_space=pl.ANY)],
            out_specs=pl.BlockSpec((1, 1, w), lambda b, pt, sel_: (b, 0, 0)),
            scratch_shapes=[pltpu.VMEM((heads, n_slots, page, d), F32),
                            pltpu.VMEM((heads, n_slots, page, d), F32),
                            pltpu.SemaphoreType.DMA((2, heads))],
        ),
        out_shape=jax.ShapeDtypeStruct((bsz, 1, w), F32),
        compiler_params=_cparams(("arbitrary",)),
        name="moba_decode",
    )(page_table, sel, slopes, z3, z3, z3, cache_k, cache_v)


def _mem_attn_kernel(q_ref, k_ref, v_ref, o_ref, *, precise):
    tq = q_ref.shape[1]
    scale = C_HEAD_DIM ** -0.5
    rows = max(tq, 8)
    for h in range(C_HEADS):
        sl = slice(h * C_HEAD_DIM, (h + 1) * C_HEAD_DIM)
        q = jnp.broadcast_to(q_ref[0, :, sl], (rows, C_HEAD_DIM)) if tq < 8 else q_ref[0, :, sl]
        s = _mm_nt(q, k_ref[0, :, sl], precise) * scale
        m = jnp.max(s, axis=1, keepdims=True)
        p = jnp.exp(s - m)
        l = jnp.sum(p, axis=1, keepdims=True)
        o = _mm(p, v_ref[0, :, sl], precise) / l
        o_ref[0, :, sl] = o[0:tq, :]


def mem_attention(z3, mem_k3, mem_v3, *, q_col, k_col, v_col, tq, precise):
    bsz, t, _ = z3.shape
    mlen = mem_k3.shape[1]
    return pl.pallas_call(
        functools.partial(_mem_attn_kernel, precise=precise),
        grid=(bsz, t // tq),
        in_specs=[pl.BlockSpec((1, tq, C_WIDTH), lambda b, i: (b, i, q_col)),
                  pl.BlockSpec((1, mlen, C_WIDTH), lambda b, i: (b, 0, k_col)),
                  pl.BlockSpec((1, mlen, C_WIDTH), lambda b, i: (b, 0, v_col))],
        out_specs=pl.BlockSpec((1, tq, C_WIDTH), lambda b, i: (b, i, 0)),
        out_shape=jax.ShapeDtypeStruct((bsz, t, C_WIDTH), F32),
        compiler_params=_cparams(("parallel", "parallel")),
        name="mem_attention",
    )(z3, mem_k3, mem_v3)


def _out_proj_kernel(ymix_ref, ymem_ref, g0_ref, g1_ref, w_ref, g_ref, x_ref, o_ref, *, precise):
    half = g0_ref.shape[1]
    wmix = ymix_ref.shape[1]
    silu = lambda t: t * _sigmoid(t)
    g1 = g1_ref[...]
    u0 = ymix_ref[:, 0:half] * silu(g0_ref[...])
    u1 = ymix_ref[:, half:wmix] * silu(g1[:, 0:wmix - half])
    u2 = ymem_ref[...] * silu(g1[:, wmix - half:])
    y = _mm(u0, w_ref[0:half, :], precise)
    y = y + _mm(u1, w_ref[half:wmix, :], precise)
    y = y + _mm(u2, w_ref[wmix:, :], precise)
    o_ref[...] = x_ref[...] + _rms_scale(y, g_ref[...])


def out_proj(ymix, ymem, z, w_out, g, x, *, gate_col, tm):
    m, dm = x.shape
    wmix = ymix.shape[1]
    half = 1024
    return pl.pallas_call(
        functools.partial(_out_proj_kernel, precise=w_out.dtype == F32),
        grid=(m // tm,),
        in_specs=[pl.BlockSpec((tm, wmix), lambda i: (i, 0)),
                  pl.BlockSpec((tm, C_WIDTH), lambda i: (i, 0)),
                  pl.BlockSpec((tm, half), lambda i: (i, gate_col)),
                  pl.BlockSpec((tm, half), lambda i: (i, gate_col + 1)),
                  pl.BlockSpec(w_out.shape, lambda i: (0, 0)),
                  pl.BlockSpec((1, dm), lambda i: (0, 0)),
                  pl.BlockSpec((tm, dm), lambda i: (i, 0))],
        out_specs=pl.BlockSpec((tm, dm), lambda i: (i, 0)),
        out_shape=jax.ShapeDtypeStruct((m, dm), F32),
        compiler_params=_cparams(("parallel",)),
        name="out_proj",
    )(ymix, ymem, z, z, w_out, g, x)


def _prep_rwkv_weights(w_in, mu, decay_b, iclr_b, vres_b, a_width):
    rkv = 3 * a_width
    n_lora = mu.shape[0] - rkv
    pad = LORA_PAD - n_lora
    dm = w_in.shape[0]
    w = jnp.concatenate([w_in[:, :rkv], w_in[:, rkv + n_lora:], w_in[:, rkv:rkv + n_lora],
                         jnp.zeros((dm, pad), w_in.dtype)], axis=1)
    mu_r = mu[:rkv].reshape(1, rkv)
    mu_l = jnp.pad(mu[rkv:], (0, pad)).reshape(1, LORA_PAD)

    def pad_rows(m, start):
        return jnp.pad(m, ((start, LORA_PAD - start - m.shape[0]), (0, 0)))

    wd = pad_rows(decay_b, 0)
    wi = pad_rows(iclr_b, DECAY_LORA)
    wv = None if vres_b is None else pad_rows(vres_b, DECAY_LORA + ICLR_LORA)
    return w, mu_r, mu_l, wd, wi, wv


def kernel(x_prompt, x_sample, state_wkv, state_shift, cache_k_l1, cache_v_l1, cache_k_l3, cache_v_l3, cache_mem_k, cache_mem_v, page_table, mem_prompt, norm_pre, norm_post, w_in_l0, w_in_l1, w_in_l2, w_in_l3, w_mem_kv, w_out, mu_l0, mu_l2, rwkv_decay_b, rwkv_decay_0, rwkv_iclr_b, rwkv_iclr_0, rwkv_k_k, rwkv_k_a, rwkv_r_k, rwkv_gn_w, rwkv_gn_b, vres_b_l2, vres_0_l2):
    depth = norm_pre.shape[0]
    bp, seq, dm = x_prompt.shape
    bs, dec_seq, _ = x_sample.shape
    assert dec_seq == 1
    a_heads = state_wkv.shape[2]
    a_width = a_heads * A_HEAD_DIM
    n_pool, page, b_heads, _ = cache_k_l1.shape
    b_width = b_heads * B_HEAD_DIM
    mem_len = mem_prompt.shape[1]
    past_len = page_table.shape[1] * page
    assert past_len % MOBA_BLOCK == 0 and past_len // MOBA_BLOCK >= MOBA_TOPK and MOBA_BLOCK % page == 0
    assert a_width == b_width
    mix_w = a_width
    qmem_col = 3 * mix_w // C_WIDTH
    gate_col = (3 * mix_w + C_WIDTH) // 1024
    lora_col = (3 * mix_w + C_WIDTH + dm) // LORA_PAD

    zeros8 = jnp.zeros((a_width,), F32)
    rw = []
    for j, (w_in, mu, vb, v0) in enumerate(((w_in_l0, mu_l0, None, None), (w_in_l2, mu_l2, vres_b_l2, vres_0_l2))):
        w, mu_r, mu_l, wd, wi, wv = _prep_rwkv_weights(w_in, mu, rwkv_decay_b[j], rwkv_iclr_b[j], vb, a_width)
        pvec = jnp.stack([rwkv_decay_0[j], rwkv_iclr_0[j], zeros8 if v0 is None else v0, rwkv_k_k[j], rwkv_k_a[j],
                          zeros8, zeros8, zeros8])
        pvec2 = jnp.stack([rwkv_r_k[j].reshape(-1), rwkv_gn_w[j], rwkv_gn_b[j]] + [zeros8] * 5)
        lora = {True: (wd.astype(BF16), wi.astype(BF16), None if wv is None else wv.astype(BF16)),
                False: (wd, wi, wv)}
        rw.append(dict(w={True: w.astype(BF16), False: w}, lora=lora, mu_r=mu_r, mu_l=mu_l, pvec=pvec, pvec2=pvec2))
    w_moba = [{True: w.astype(BF16), False: w} for w in (w_in_l1, w_in_l3)]
    w_out_p = {True: w_out.astype(BF16), False: w_out}
    w_mem_all = jnp.transpose(w_mem_kv, (1, 0, 2)).reshape(dm, depth * 2 * C_WIDTH).astype(BF16)
    slopes = jnp.asarray(alibi_slopes(b_heads))
    ones_g = jnp.ones((1, dm), F32)
    caches = [tuple(jnp.transpose(c, (0, 2, 1, 3)) for c in pair)
              for pair in ((cache_k_l1, cache_v_l1), (cache_k_l3, cache_v_l3))]

    mem_rows = mem_prompt.reshape(bp * mem_len, dm)
    memkv = norm_matmul(mem_rows, ones_g, w_mem_all, normalize=False, tm=bp * mem_len, tn=1024)
    memkv3 = memkv.reshape(bp, mem_len, depth * 2 * C_WIDTH)
    memkv5 = memkv.reshape(bp, mem_len, depth, 2, C_HEADS, C_HEAD_DIM)
    p_mem_k = jnp.transpose(memkv5[:, :, :, 0], (2, 0, 1, 3, 4))
    p_mem_v = jnp.transpose(memkv5[:, :, :, 1], (2, 0, 1, 3, 4))
    cache_mem_k3 = cache_mem_k.reshape(depth, bs, mem_len, C_WIDTH)
    cache_mem_v3 = cache_mem_v.reshape(depth, bs, mem_len, C_WIDTH)

    ksums = {}

    def trunk(x3, is_prompt):
        bsz, t, _ = x3.shape
        m = bsz * t
        x = x3.reshape(m, dm)
        tm_a = 1024 if m % 1024 == 0 else m
        tm_f = 256 if m % 256 == 0 else m
        v_first = None
        wkv_out, shift_out, kv_out = [], [], []
        for i in range(depth):
            j = i // 2
            g_pre = norm_pre[i].reshape(1, dm)
            if i % 2 == 0:
                p = rw[j]
                z = norm_matmul(x, g_pre, p["w"][is_prompt], normalize=True, tm=tm_a, tn=768)
                z3 = z.reshape(bsz, t, -1)
                if is_prompt:
                    zprev3 = jnp.zeros((bsz, 1, z.shape[1]), F32)
                else:
                    zprev3 = norm_matmul(state_shift[j], ones_g, p["w"][is_prompt], normalize=False, tm=bsz, tn=768
                                         ).reshape(bsz, 1, -1)
                wd, wi, wv = p["lora"][is_prompt]
                r, lw, k, v, a, b = rwkv_prep(z3, zprev3, p["mu_r"], p["mu_l"], wd, wi, wv, p["pvec"],
                                              v_first, tt=min(t, 128), width=a_width, lora_col=lora_col)
                if v_first is None:
                    v_first = v
                if is_prompt:
                    y_mix, h_bd, ksums[j] = rwkv_chunk_scan(r, lw, k, v, a, b, p["pvec2"], caches[j][0], page_table,
                                                            pages_per_block=MOBA_BLOCK // page)
                    hb = h_bd.reshape(bsz, a_heads // 2, 2, A_HEAD_DIM, 2, A_HEAD_DIM)
                    s_new = jnp.stack([hb[:, :, 0, :, 0, :], hb[:, :, 1, :, 1, :]], axis=2)
                    s_new = jnp.swapaxes(s_new.reshape(bsz, a_heads, A_HEAD_DIM, A_HEAD_DIM), -1, -2)
                else:
                    y_mix, s_new = rwkv_step(state_wkv[j], r, lw, k, v, a, b, rwkv_r_k[j], rwkv_gn_w[j], rwkv_gn_b[j])
                wkv_out.append(s_new)
                shift_out.append(rmsnorm_rows(x3[:, -1], g_pre))
            else:
                z = norm_matmul(x, g_pre, w_moba[j][is_prompt], normalize=True, tm=tm_a, tn=1024)
                z3 = z.reshape(bsz, t, -1)
                if is_prompt:
                    y_mix, k_hm, v_hm = moba_prompt(z3, slopes, heads=b_heads)
                    kv_out.append((jnp.transpose(k_hm, (0, 2, 1, 3)), jnp.transpose(v_hm, (0, 2, 1, 3))))
                else:
                    ck, cv = caches[j]
                    sel = moba_select(ksums[j], z3[:, 0, :b_width].reshape(bsz, b_heads, B_HEAD_DIM),
                                      block_len=MOBA_BLOCK)
                    y_mix = moba_decode(z3, ck, cv, page_table, sel[..., 0], slopes, past_len=past_len)
                    kv_out.append((z3[:, :, b_width:2 * b_width].reshape(bsz, t, b_heads, B_HEAD_DIM),
                                   z3[:, :, 2 * b_width:3 * b_width].reshape(bsz, t, b_heads, B_HEAD_DIM)))
            if is_prompt:
                y_mem = mem_attention(z3, memkv3, memkv3, q_col=qmem_col, k_col=2 * i, v_col=2 * i + 1,
                                      tq=min(t, 512), precise=False)
            else:
                y_mem = mem_attention(z3, cache_mem_k3[i], cache_mem_v3[i], q_col=qmem_col, k_col=0, v_col=0, tq=t,
                                      precise=True)
            x = out_proj(y_mix.reshape(m, mix_w), y_mem.reshape(m, C_WIDTH), z, w_out_p[is_prompt][i],
                         norm_post[i].reshape(1, dm), x, gate_col=gate_col, tm=tm_f)
            x3 = x.reshape(bsz, t, dm)
        return x3, jnp.stack(wkv_out), jnp.stack(shift_out), kv_out

    y_prompt, p_wkv, p_shift, p_kv = trunk(x_prompt, True)
    y_sample, s_wkv, s_shift, s_kv = trunk(x_sample, False)
    (p_k_l1, p_v_l1), (p_k_l3, p_v_l3) = p_kv
    (s_k_l1, s_v_l1), (s_k_l3, s_v_l3) = s_kv
    return (y_prompt, y_sample, p_wkv, p_shift, p_k_l1, p_v_l1, p_k_l3, p_v_l3, p_mem_k, p_mem_v,
            s_wkv, s_shift, s_k_l1, s_v_l1, s_k_l3, s_v_l3)
```
